```python
import jax, jax.numpy as jnp
from jax import lax
import numpy as np

D_MODEL = 1024
BATCH = 8
SEQ = 8192
DEPTH = 4

N_MIXERS = 3
N_A = (DEPTH + 2) // 3
N_B = (DEPTH + 1) // 3
N_C = DEPTH // 3
D_RNN = 1280
A_HEADS = 16
A_HEAD_DIM = D_RNN // A_HEADS
A_CONV = 4
LRU_C = 8.0
D_SGU = D_MODEL
SGU_CHUNK = 128
SGU_GROUPS = 8
SGU_GROUP_DIM = D_SGU // SGU_GROUPS
D_CONV = D_MODEL
C_CONV = 3
D_FF = 4 * D_MODEL
EPS = 1e-6

kernel_name = "hybrid_rglru_sgu_shortconv_trunk"


def _rmsnorm(x, g):
    x32 = x.astype(jnp.float32)
    y = x32 * lax.rsqrt(jnp.mean(x32 * x32, axis=-1, keepdims=True) + EPS)
    return y.astype(x.dtype) * g


def _causal_dwconv(x, w):
    k = w.shape[0]
    return lax.conv_general_dilated(
        x, w[:, None, :].astype(x.dtype), window_strides=(1,),
        padding=[(k - 1, 0)], dimension_numbers=("NWC", "WIO", "NWC"),
        feature_group_count=x.shape[-1])


def _lru_combine(left, right):
    a_l, b_l = left
    a_r, b_r = right
    return a_l * a_r, a_r * b_l + b_r


def _rglru_mixer(h, w_in, conv_w, conv_b, gate_a_w, gate_a_b, gate_x_w, gate_x_b, lam, w_out):
    b, s, _ = h.shape
    gate, xr = jnp.split(h @ w_in, 2, axis=-1)
    gate = jax.nn.gelu(gate)
    xr = _causal_dwconv(xr, conv_w) + conv_b
    xh = xr.reshape(b, s, A_HEADS, A_HEAD_DIM)
    r = jax.nn.sigmoid(jnp.einsum("bshi,hij->bshj", xh, gate_a_w).reshape(b, s, D_RNN) + gate_a_b)
    ig = jax.nn.sigmoid(jnp.einsum("bshi,hij->bshj", xh, gate_x_w).reshape(b, s, D_RNN) + gate_x_b)
    log_a = -LRU_C * r.astype(jnp.float32) * jax.nn.softplus(-lam.astype(jnp.float32))
    a = jnp.exp(log_a)
    u = jnp.sqrt(-jnp.expm1(2.0 * log_a)) * (ig * xr).astype(jnp.float32)
    _, hs = lax.associative_scan(_lru_combine, (a, u), axis=1)
    return (hs.astype(h.dtype) * gate) @ w_out


def _sgu_mixer(h, w_in, norm_g, w_s, s_bias, w_out):
    b, s, _ = h.shape
    z = jax.nn.gelu(h @ w_in)
    u, v = jnp.split(z, 2, axis=-1)
    v = _rmsnorm(v, norm_g)
    n_chunks = s // SGU_CHUNK
    v = v.reshape(b, n_chunks, SGU_CHUNK, SGU_GROUPS, SGU_GROUP_DIM)
    causal = jnp.tril(jnp.ones((SGU_CHUNK, SGU_CHUNK), dtype=bool))
    w_causal = jnp.where(causal[None], w_s, jnp.zeros((), w_s.dtype))
    mixed = jnp.einsum("gts,bnsgc->bntgc", w_causal, v) + s_bias.T[None, None, :, :, None]
    y = u * mixed.reshape(b, s, D_SGU)
    return y @ w_out


def _shortconv_mixer(h, w_in, conv_w, w_out):
    gb, gc, xv = jnp.split(h @ w_in, 3, axis=-1)
    y = gb * _causal_dwconv(gc * xv, conv_w)
    return y @ w_out


def _sqrelu_mlp(h, w1, w2):
    return jnp.square(jax.nn.relu(h @ w1)) @ w2


def _fwd_setup_inputs(seed: int = 0) -> dict:
    key = jax.random.key(seed)
    ks = jax.random.split(key, 24)
    f32 = jnp.float32

    def w(k, shape, fan_in):
        return jax.random.normal(k, shape, f32) * (fan_in ** -0.5)

    def gain(k, shape):
        return 1.0 + 0.1 * jax.random.normal(k, shape, f32)

    def bias(k, shape, scale=0.1):
        return scale * jax.random.normal(k, shape, f32)

    a_c = jax.random.uniform(ks[11], (N_A, D_RNN), f32, minval=0.9, maxval=0.999)
    a0 = a_c ** (1.0 / LRU_C)
    a_lambda = jnp.log(a0) - jnp.log1p(-a0)

    return {
        "x": jax.random.normal(ks[0], (BATCH, SEQ, D_MODEL), f32),
        "norm_mix_g": gain(ks[1], (DEPTH, D_MODEL)),
        "norm_mlp_g": gain(ks[2], (DEPTH, D_MODEL)),
        "final_norm_g": gain(ks[3], (D_MODEL,)),
        "a_w_in": w(ks[4], (N_A, D_MODEL, 2 * D_RNN), D_MODEL),
        "a_conv_w": w(ks[5], (N_A, A_CONV, D_RNN), A_CONV),
        "a_conv_b": bias(ks[6], (N_A, D_RNN)),
        "a_gate_a_w": w(ks[7], (N_A, A_HEADS, A_HEAD_DIM, A_HEAD_DIM), A_HEAD_DIM),
        "a_gate_a_b": bias(ks[8], (N_A, D_RNN)),
        "a_gate_x_w": w(ks[9], (N_A, A_HEADS, A_HEAD_DIM, A_HEAD_DIM), A_HEAD_DIM),
        "a_gate_x_b": bias(ks[10], (N_A, D_RNN)),
        "a_lambda": a_lambda,
        "a_w_out": w(ks[12], (N_A, D_RNN, D_MODEL), D_RNN),
        "b_w_in": w(ks[13], (N_B, D_MODEL, 2 * D_SGU), D_MODEL),
        "b_norm_g": gain(ks[14], (N_B, D_SGU)),
        "b_w_s": w(ks[15], (N_B, SGU_GROUPS, SGU_CHUNK, SGU_CHUNK), SGU_CHUNK),
        "b_s_bias": gain(ks[16], (N_B, SGU_GROUPS, SGU_CHUNK)),
        "b_w_out": w(ks[17], (N_B, D_SGU, D_MODEL), D_SGU),
        "c_w_in": w(ks[18], (N_C, D_MODEL, 3 * D_CONV), D_MODEL),
        "c_conv_w": w(ks[19], (N_C, C_CONV, D_CONV), C_CONV),
        "c_w_out": w(ks[20], (N_C, D_CONV, D_MODEL), D_CONV),
        "mlp_w1": w(ks[21], (DEPTH, D_MODEL, D_FF), D_MODEL),
        "mlp_w2": w(ks[22], (DEPTH, D_FF, D_MODEL), D_FF),
    }


def _fwd_reference(x, norm_mix_g, norm_mlp_g, final_norm_g,
              a_w_in, a_conv_w, a_conv_b, a_gate_a_w, a_gate_a_b, a_gate_x_w, a_gate_x_b, a_lambda, a_w_out,
              b_w_in, b_norm_g, b_w_s, b_s_bias, b_w_out,
              c_w_in, c_conv_w, c_w_out,
              mlp_w1, mlp_w2):
    for i in range(DEPTH):
        kind, j = i % N_MIXERS, i // N_MIXERS
        h = _rmsnorm(x, norm_mix_g[i])
        if kind == 0:
            mix = _rglru_mixer(h, a_w_in[j], a_conv_w[j], a_conv_b[j], a_gate_a_w[j], a_gate_a_b[j],
                               a_gate_x_w[j], a_gate_x_b[j], a_lambda[j], a_w_out[j])
        elif kind == 1:
            mix = _sgu_mixer(h, b_w_in[j], b_norm_g[j], b_w_s[j], b_s_bias[j], b_w_out[j])
        else:
            mix = _shortconv_mixer(h, c_w_in[j], c_conv_w[j], c_w_out[j])
        x = x + mix
        x = x + _sqrelu_mlp(_rmsnorm(x, norm_mlp_g[i]), mlp_w1[i], mlp_w2[i])
    return _rmsnorm(x, final_norm_g)


import jax as _jax
import jax.numpy as _jnp

TWIN_FORMAT = 'train_step'
FWD_PARAMS = ['x', 'norm_mix_g', 'norm_mlp_g', 'final_norm_g', 'a_w_in', 'a_conv_w', 'a_conv_b', 'a_gate_a_w', 'a_gate_a_b', 'a_gate_x_w', 'a_gate_x_b', 'a_lambda', 'a_w_out', 'b_w_in', 'b_norm_g', 'b_w_s', 'b_s_bias', 'b_w_out', 'c_w_in', 'c_conv_w', 'c_w_out', 'mlp_w1', 'mlp_w2']
TWIN_WEIGHTS = ['norm_mix_g', 'norm_mlp_g', 'final_norm_g', 'a_w_in', 'a_conv_w', 'a_conv_b', 'a_gate_a_w', 'a_gate_a_b', 'a_gate_x_w', 'a_gate_x_b', 'a_lambda', 'a_w_out', 'b_w_in', 'b_norm_g', 'b_w_s', 'b_s_bias', 'b_w_out', 'c_w_in', 'c_conv_w', 'c_w_out', 'mlp_w1', 'mlp_w2']
TWIN_DIFF_INPUT = 'x'
TWIN_INPUTS = ['x', 'norm_mix_g', 'norm_mlp_g', 'final_norm_g', 'a_w_in', 'a_conv_w', 'a_conv_b', 'a_gate_a_w', 'a_gate_a_b', 'a_gate_x_w', 'a_gate_x_b', 'a_lambda', 'a_w_out', 'b_w_in', 'b_norm_g', 'b_w_s', 'b_s_bias', 'b_w_out', 'c_w_in', 'c_conv_w', 'c_w_out', 'mlp_w1', 'mlp_w2', 'loss_target', 'm_norm_mix_g', 'm_norm_mlp_g', 'm_final_norm_g', 'm_a_w_in', 'm_a_conv_w', 'm_a_conv_b', 'm_a_gate_a_w', 'm_a_gate_a_b', 'm_a_gate_x_w', 'm_a_gate_x_b', 'm_a_lambda', 'm_a_w_out', 'm_b_w_in', 'm_b_norm_g', 'm_b_w_s', 'm_b_s_bias', 'm_b_w_out', 'm_c_w_in', 'm_c_conv_w', 'm_c_w_out', 'm_mlp_w1', 'm_mlp_w2', 'v_norm_mix_g', 'v_norm_mlp_g', 'v_final_norm_g', 'v_a_w_in', 'v_a_conv_w', 'v_a_conv_b', 'v_a_gate_a_w', 'v_a_gate_a_b', 'v_a_gate_x_w', 'v_a_gate_x_b', 'v_a_lambda', 'v_a_w_out', 'v_b_w_in', 'v_b_norm_g', 'v_b_w_s', 'v_b_s_bias', 'v_b_w_out', 'v_c_w_in', 'v_c_conv_w', 'v_c_w_out', 'v_mlp_w1', 'v_mlp_w2']
TWIN_OUTPUTS = ['loss', 'grad_x', 'grad_norm_mix_g', 'grad_norm_mlp_g', 'grad_final_norm_g', 'grad_a_w_in', 'grad_a_conv_w', 'grad_a_conv_b', 'grad_a_gate_a_w', 'grad_a_gate_a_b', 'grad_a_gate_x_w', 'grad_a_gate_x_b', 'grad_a_lambda', 'grad_a_w_out', 'grad_b_w_in', 'grad_b_norm_g', 'grad_b_w_s', 'grad_b_s_bias', 'grad_b_w_out', 'grad_c_w_in', 'grad_c_conv_w', 'grad_c_w_out', 'grad_mlp_w1', 'grad_mlp_w2', 'delta_norm_mix_g', 'delta_norm_mlp_g', 'delta_final_norm_g', 'delta_a_w_in', 'delta_a_conv_w', 'delta_a_conv_b', 'delta_a_gate_a_w', 'delta_a_gate_a_b', 'delta_a_gate_x_w', 'delta_a_gate_x_b', 'delta_a_lambda', 'delta_a_w_out', 'delta_b_w_in', 'delta_b_norm_g', 'delta_b_w_s', 'delta_b_s_bias', 'delta_b_w_out', 'delta_c_w_in', 'delta_c_conv_w', 'delta_c_w_out', 'delta_mlp_w1', 'delta_mlp_w2', 'new_m_norm_mix_g', 'new_m_norm_mlp_g', 'new_m_final_norm_g', 'new_m_a_w_in', 'new_m_a_conv_w', 'new_m_a_conv_b', 'new_m_a_gate_a_w', 'new_m_a_gate_a_b', 'new_m_a_gate_x_w', 'new_m_a_gate_x_b', 'new_m_a_lambda', 'new_m_a_w_out', 'new_m_b_w_in', 'new_m_b_norm_g', 'new_m_b_w_s', 'new_m_b_s_bias', 'new_m_b_w_out', 'new_m_c_w_in', 'new_m_c_conv_w', 'new_m_c_w_out', 'new_m_mlp_w1', 'new_m_mlp_w2', 'new_v_norm_mix_g', 'new_v_norm_mlp_g', 'new_v_final_norm_g', 'new_v_a_w_in', 'new_v_a_conv_w', 'new_v_a_conv_b', 'new_v_a_gate_a_w', 'new_v_a_gate_a_b', 'new_v_a_gate_x_w', 'new_v_a_gate_x_b', 'new_v_a_lambda', 'new_v_a_w_out', 'new_v_b_w_in', 'new_v_b_norm_g', 'new_v_b_w_s', 'new_v_b_s_bias', 'new_v_b_w_out', 'new_v_c_w_in', 'new_v_c_conv_w', 'new_v_c_w_out', 'new_v_mlp_w1', 'new_v_mlp_w2']
TWIN_LEAF_KINDS = {'loss': 'loss', 'grad_x': 'grad_x', 'grad_norm_mix_g': 'grad_w', 'grad_norm_mlp_g': 'grad_w', 'grad_final_norm_g': 'grad_w', 'grad_a_w_in': 'grad_w', 'grad_a_conv_w': 'grad_w', 'grad_a_conv_b': 'grad_w', 'grad_a_gate_a_w': 'grad_w', 'grad_a_gate_a_b': 'grad_w', 'grad_a_gate_x_w': 'grad_w', 'grad_a_gate_x_b': 'grad_w', 'grad_a_lambda': 'grad_w', 'grad_a_w_out': 'grad_w', 'grad_b_w_in': 'grad_w', 'grad_b_norm_g': 'grad_w', 'grad_b_w_s': 'grad_w', 'grad_b_s_bias': 'grad_w', 'grad_b_w_out': 'grad_w', 'grad_c_w_in': 'grad_w', 'grad_c_conv_w': 'grad_w', 'grad_c_w_out': 'grad_w', 'grad_mlp_w1': 'grad_w', 'grad_mlp_w2': 'grad_w', 'delta_norm_mix_g': 'delta_w', 'delta_norm_mlp_g': 'delta_w', 'delta_final_norm_g': 'delta_w', 'delta_a_w_in': 'delta_w', 'delta_a_conv_w': 'delta_w', 'delta_a_conv_b': 'delta_w', 'delta_a_gate_a_w': 'delta_w', 'delta_a_gate_a_b': 'delta_w', 'delta_a_gate_x_w': 'delta_w', 'delta_a_gate_x_b': 'delta_w', 'delta_a_lambda': 'delta_w', 'delta_a_w_out': 'delta_w', 'delta_b_w_in': 'delta_w', 'delta_b_norm_g': 'delta_w', 'delta_b_w_s': 'delta_w', 'delta_b_s_bias': 'delta_w', 'delta_b_w_out': 'delta_w', 'delta_c_w_in': 'delta_w', 'delta_c_conv_w': 'delta_w', 'delta_c_w_out': 'delta_w', 'delta_mlp_w1': 'delta_w', 'delta_mlp_w2': 'delta_w', 'new_m_norm_mix_g': 'new_m', 'new_m_norm_mlp_g': 'new_m', 'new_m_final_norm_g': 'new_m', 'new_m_a_w_in': 'new_m', 'new_m_a_conv_w': 'new_m', 'new_m_a_conv_b': 'new_m', 'new_m_a_gate_a_w': 'new_m', 'new_m_a_gate_a_b': 'new_m', 'new_m_a_gate_x_w': 'new_m', 'new_m_a_gate_x_b': 'new_m', 'new_m_a_lambda': 'new_m', 'new_m_a_w_out': 'new_m', 'new_m_b_w_in': 'new_m', 'new_m_b_norm_g': 'new_m', 'new_m_b_w_s': 'new_m', 'new_m_b_s_bias': 'new_m', 'new_m_b_w_out': 'new_m', 'new_m_c_w_in': 'new_m', 'new_m_c_conv_w': 'new_m', 'new_m_c_w_out': 'new_m', 'new_m_mlp_w1': 'new_m', 'new_m_mlp_w2': 'new_m', 'new_v_norm_mix_g': 'new_v', 'new_v_norm_mlp_g': 'new_v', 'new_v_final_norm_g': 'new_v', 'new_v_a_w_in': 'new_v', 'new_v_a_conv_w': 'new_v', 'new_v_a_conv_b': 'new_v', 'new_v_a_gate_a_w': 'new_v', 'new_v_a_gate_a_b': 'new_v', 'new_v_a_gate_x_w': 'new_v', 'new_v_a_gate_x_b': 'new_v', 'new_v_a_lambda': 'new_v', 'new_v_a_w_out': 'new_v', 'new_v_b_w_in': 'new_v', 'new_v_b_norm_g': 'new_v', 'new_v_b_w_s': 'new_v', 'new_v_b_s_bias': 'new_v', 'new_v_b_w_out': 'new_v', 'new_v_c_w_in': 'new_v', 'new_v_c_conv_w': 'new_v', 'new_v_c_w_out': 'new_v', 'new_v_mlp_w1': 'new_v', 'new_v_mlp_w2': 'new_v'}


def _forward(args):
    return _fwd_reference(*[args[k] for k in FWD_PARAMS])


def _output_shape():
    def fwd():
        inp = _fwd_setup_inputs(0)
        return _fwd_reference(*[inp[k] for k in FWD_PARAMS])
    out = _jax.eval_shape(fwd)
    return out.shape, out.dtype

N_MICROBATCH = 1
ADAM_LR = 0.001
ADAM_B1 = 0.9
ADAM_B2 = 0.999
ADAM_EPS = 1e-08
ADAM_WD = 0.01
ADAM_STEP = 10
PER_EXAMPLE_BATCH_AXIS = {'x': 0, 'loss_target': 0}
SHARED_INPUTS = []
_WEIGHT_DTYPES = {'norm_mix_g': _jnp.float32, 'norm_mlp_g': _jnp.float32, 'final_norm_g': _jnp.float32, 'a_w_in': _jnp.float32, 'a_conv_w': _jnp.float32, 'a_conv_b': _jnp.float32, 'a_gate_a_w': _jnp.float32, 'a_gate_a_b': _jnp.float32, 'a_gate_x_w': _jnp.float32, 'a_gate_x_b': _jnp.float32, 'a_lambda': _jnp.float32, 'a_w_out': _jnp.float32, 'b_w_in': _jnp.float32, 'b_norm_g': _jnp.float32, 'b_w_s': _jnp.float32, 'b_s_bias': _jnp.float32, 'b_w_out': _jnp.float32, 'c_w_in': _jnp.float32, 'c_conv_w': _jnp.float32, 'c_w_out': _jnp.float32, 'mlp_w1': _jnp.float32, 'mlp_w2': _jnp.float32}
MOMENT_SCALE = {'norm_mix_g': 6.320265e-01, 'norm_mlp_g': 4.443051e-01, 'final_norm_g': 6.613870e+01, 'a_w_in': 6.393325e-01, 'a_conv_w': 8.058584e-01, 'a_conv_b': 4.065463e+00, 'a_gate_a_w': 1.258589e-01, 'a_gate_a_b': 1.362643e-01, 'a_gate_x_w': 2.642250e-01, 'a_gate_x_b': 3.489463e-01, 'a_lambda': 2.993805e-01, 'a_w_out': 9.554877e-01, 'b_w_in': 2.262357e-01, 'b_norm_g': 8.457114e-02, 'b_w_s': 8.513867e-02, 'b_s_bias': 1.276070e-01, 'b_w_out': 6.941440e-01, 'c_w_in': 1.328343e-01, 'c_conv_w': 1.386039e-01, 'c_w_out': 1.405189e-01, 'mlp_w1': 2.132944e-01, 'mlp_w2': 1.071649e+00}


def _to_microbatches(a, axis):
    t = _jnp.moveaxis(a, axis, 0)
    t = t.reshape((N_MICROBATCH, t.shape[0] // N_MICROBATCH) + t.shape[1:])
    return _jnp.moveaxis(t, 1, axis + 1)


def setup_inputs(seed: int = 0) -> dict:
    inp = _fwd_setup_inputs(seed)
    key = _jax.random.fold_in(_jax.random.key(seed), 7919)
    shape, _ = _output_shape()
    out = dict(inp)
    out["loss_target"] = _jax.random.normal(_jax.random.fold_in(key, 0), shape, _jnp.float32)
    for i, name in enumerate(TWIN_WEIGHTS):
        w = inp[name].astype(_jnp.float32)
        if MOMENT_SCALE is None:
            s = _jnp.sqrt(_jnp.mean(_jnp.square(w)) + 1e-30)
        else:
            s = MOMENT_SCALE[name]
        km, kv = _jax.random.split(_jax.random.fold_in(key, i + 1))
        out[name] = w
        out["m_" + name] = s * _jax.random.normal(km, w.shape, _jnp.float32)
        out["v_" + name] = (s * s) * _jax.random.uniform(kv, w.shape, _jnp.float32, 0.5, 1.5)
    if N_MICROBATCH > 1:
        for name, axis in PER_EXAMPLE_BATCH_AXIS.items():
            out[name] = _to_microbatches(out[name], axis)
    return {'x': out['x'], 'norm_mix_g': out['norm_mix_g'], 'norm_mlp_g': out['norm_mlp_g'], 'final_norm_g': out['final_norm_g'], 'a_w_in': out['a_w_in'], 'a_conv_w': out['a_conv_w'], 'a_conv_b': out['a_conv_b'], 'a_gate_a_w': out['a_gate_a_w'], 'a_gate_a_b': out['a_gate_a_b'], 'a_gate_x_w': out['a_gate_x_w'], 'a_gate_x_b': out['a_gate_x_b'], 'a_lambda': out['a_lambda'], 'a_w_out': out['a_w_out'], 'b_w_in': out['b_w_in'], 'b_norm_g': out['b_norm_g'], 'b_w_s': out['b_w_s'], 'b_s_bias': out['b_s_bias'], 'b_w_out': out['b_w_out'], 'c_w_in': out['c_w_in'], 'c_conv_w': out['c_conv_w'], 'c_w_out': out['c_w_out'], 'mlp_w1': out['mlp_w1'], 'mlp_w2': out['mlp_w2'], 'loss_target': out['loss_target'], 'm_norm_mix_g': out['m_norm_mix_g'], 'm_norm_mlp_g': out['m_norm_mlp_g'], 'm_final_norm_g': out['m_final_norm_g'], 'm_a_w_in': out['m_a_w_in'], 'm_a_conv_w': out['m_a_conv_w'], 'm_a_conv_b': out['m_a_conv_b'], 'm_a_gate_a_w': out['m_a_gate_a_w'], 'm_a_gate_a_b': out['m_a_gate_a_b'], 'm_a_gate_x_w': out['m_a_gate_x_w'], 'm_a_gate_x_b': out['m_a_gate_x_b'], 'm_a_lambda': out['m_a_lambda'], 'm_a_w_out': out['m_a_w_out'], 'm_b_w_in': out['m_b_w_in'], 'm_b_norm_g': out['m_b_norm_g'], 'm_b_w_s': out['m_b_w_s'], 'm_b_s_bias': out['m_b_s_bias'], 'm_b_w_out': out['m_b_w_out'], 'm_c_w_in': out['m_c_w_in'], 'm_c_conv_w': out['m_c_conv_w'], 'm_c_w_out': out['m_c_w_out'], 'm_mlp_w1': out['m_mlp_w1'], 'm_mlp_w2': out['m_mlp_w2'], 'v_norm_mix_g': out['v_norm_mix_g'], 'v_norm_mlp_g': out['v_norm_mlp_g'], 'v_final_norm_g': out['v_final_norm_g'], 'v_a_w_in': out['v_a_w_in'], 'v_a_conv_w': out['v_a_conv_w'], 'v_a_conv_b': out['v_a_conv_b'], 'v_a_gate_a_w': out['v_a_gate_a_w'], 'v_a_gate_a_b': out['v_a_gate_a_b'], 'v_a_gate_x_w': out['v_a_gate_x_w'], 'v_a_gate_x_b': out['v_a_gate_x_b'], 'v_a_lambda': out['v_a_lambda'], 'v_a_w_out': out['v_a_w_out'], 'v_b_w_in': out['v_b_w_in'], 'v_b_norm_g': out['v_b_norm_g'], 'v_b_w_s': out['v_b_w_s'], 'v_b_s_bias': out['v_b_s_bias'], 'v_b_w_out': out['v_b_w_out'], 'v_c_w_in': out['v_c_w_in'], 'v_c_conv_w': out['v_c_conv_w'], 'v_c_w_out': out['v_c_w_out'], 'v_mlp_w1': out['v_mlp_w1'], 'v_mlp_w2': out['v_mlp_w2']}


def _loss(weights, diff, rest, loss_target):
    with _jax.named_scope("forward"):
        args = {**rest, TWIN_DIFF_INPUT: diff, **{k: w.astype(_WEIGHT_DTYPES[k]) for k, w in weights.items()}}
        y = _forward(args)
    with _jax.named_scope("loss_head"):
        err = _jnp.square(y.astype(_jnp.float32) - loss_target)
        return 0.5 * _jnp.sum(_jnp.mean(err, axis=-1)) if err.ndim else 0.5 * err


def _adamw(w, g, m, v):
    m = ADAM_B1 * m + (1.0 - ADAM_B1) * g
    v = ADAM_B2 * v + (1.0 - ADAM_B2) * _jnp.square(g)
    m_hat = m / (1.0 - ADAM_B1 ** ADAM_STEP)
    v_hat = v / (1.0 - ADAM_B2 ** ADAM_STEP)
    delta = -ADAM_LR * (m_hat / (_jnp.sqrt(v_hat) + ADAM_EPS) + ADAM_WD * w)
    return delta, m, v


def reference(x, norm_mix_g, norm_mlp_g, final_norm_g, a_w_in, a_conv_w, a_conv_b, a_gate_a_w, a_gate_a_b, a_gate_x_w, a_gate_x_b, a_lambda, a_w_out, b_w_in, b_norm_g, b_w_s, b_s_bias, b_w_out, c_w_in, c_conv_w, c_w_out, mlp_w1, mlp_w2, loss_target, m_norm_mix_g, m_norm_mlp_g, m_final_norm_g, m_a_w_in, m_a_conv_w, m_a_conv_b, m_a_gate_a_w, m_a_gate_a_b, m_a_gate_x_w, m_a_gate_x_b, m_a_lambda, m_a_w_out, m_b_w_in, m_b_norm_g, m_b_w_s, m_b_s_bias, m_b_w_out, m_c_w_in, m_c_conv_w, m_c_w_out, m_mlp_w1, m_mlp_w2, v_norm_mix_g, v_norm_mlp_g, v_final_norm_g, v_a_w_in, v_a_conv_w, v_a_conv_b, v_a_gate_a_w, v_a_gate_a_b, v_a_gate_x_w, v_a_gate_x_b, v_a_lambda, v_a_w_out, v_b_w_in, v_b_norm_g, v_b_w_s, v_b_s_bias, v_b_w_out, v_c_w_in, v_c_conv_w, v_c_w_out, v_mlp_w1, v_mlp_w2):
    given = dict(x=x, norm_mix_g=norm_mix_g, norm_mlp_g=norm_mlp_g, final_norm_g=final_norm_g, a_w_in=a_w_in, a_conv_w=a_conv_w, a_conv_b=a_conv_b, a_gate_a_w=a_gate_a_w, a_gate_a_b=a_gate_a_b, a_gate_x_w=a_gate_x_w, a_gate_x_b=a_gate_x_b, a_lambda=a_lambda, a_w_out=a_w_out, b_w_in=b_w_in, b_norm_g=b_norm_g, b_w_s=b_w_s, b_s_bias=b_s_bias, b_w_out=b_w_out, c_w_in=c_w_in, c_conv_w=c_conv_w, c_w_out=c_w_out, mlp_w1=mlp_w1, mlp_w2=mlp_w2, loss_target=loss_target, m_norm_mix_g=m_norm_mix_g, m_norm_mlp_g=m_norm_mlp_g, m_final_norm_g=m_final_norm_g, m_a_w_in=m_a_w_in, m_a_conv_w=m_a_conv_w, m_a_conv_b=m_a_conv_b, m_a_gate_a_w=m_a_gate_a_w, m_a_gate_a_b=m_a_gate_a_b, m_a_gate_x_w=m_a_gate_x_w, m_a_gate_x_b=m_a_gate_x_b, m_a_lambda=m_a_lambda, m_a_w_out=m_a_w_out, m_b_w_in=m_b_w_in, m_b_norm_g=m_b_norm_g, m_b_w_s=m_b_w_s, m_b_s_bias=m_b_s_bias, m_b_w_out=m_b_w_out, m_c_w_in=m_c_w_in, m_c_conv_w=m_c_conv_w, m_c_w_out=m_c_w_out, m_mlp_w1=m_mlp_w1, m_mlp_w2=m_mlp_w2, v_norm_mix_g=v_norm_mix_g, v_norm_mlp_g=v_norm_mlp_g, v_final_norm_g=v_final_norm_g, v_a_w_in=v_a_w_in, v_a_conv_w=v_a_conv_w, v_a_conv_b=v_a_conv_b, v_a_gate_a_w=v_a_gate_a_w, v_a_gate_a_b=v_a_gate_a_b, v_a_gate_x_w=v_a_gate_x_w, v_a_gate_x_b=v_a_gate_x_b, v_a_lambda=v_a_lambda, v_a_w_out=v_a_w_out, v_b_w_in=v_b_w_in, v_b_norm_g=v_b_norm_g, v_b_w_s=v_b_w_s, v_b_s_bias=v_b_s_bias, v_b_w_out=v_b_w_out, v_c_w_in=v_c_w_in, v_c_conv_w=v_c_conv_w, v_c_w_out=v_c_w_out, v_mlp_w1=v_mlp_w1, v_mlp_w2=v_mlp_w2)
    weights = {n: given[n] for n in TWIN_WEIGHTS}
    shared = {n: given[n] for n in SHARED_INPUTS}
    per_example = {n: given[n] for n in ['x']}
    grad_fn = _jax.value_and_grad(_loss, argnums=(0, 1))

    def one_microbatch(ex, loss_target):
        ex = dict(ex)
        diff = ex.pop(TWIN_DIFF_INPUT)
        return grad_fn(weights, diff, {**shared, **ex}, loss_target)

    if N_MICROBATCH == 1:
        loss, (grad_w, grad_x) = one_microbatch(per_example, given["loss_target"])
    else:
        def body(carry, xs):
            loss_sum, grad_sum = carry
            l_k, (gw_k, gx_k) = one_microbatch(xs[0], xs[1])
            with _jax.named_scope("update"):
                return (loss_sum + l_k, _jax.tree.map(_jnp.add, grad_sum, gw_k)), gx_k

        init = (_jnp.zeros((), _jnp.float32), _jax.tree.map(_jnp.zeros_like, weights))
        (loss, grad_w), grad_x = _jax.lax.scan(body, init, (per_example, given["loss_target"]))
    with _jax.named_scope("update"):
        delta_w, new_m, new_v = {}, {}, {}
        for n in TWIN_WEIGHTS:
            delta_w[n], new_m[n], new_v[n] = _adamw(weights[n], grad_w[n], given["m_" + n], given["v_" + n])
    return (loss, grad_x, *[grad_w[n] for n in TWIN_WEIGHTS], *[delta_w[n] for n in TWIN_WEIGHTS],
            *[new_m[n] for n in TWIN_WEIGHTS], *[new_v[n] for n in TWIN_WEIGHTS])
```

```python
import functools
import math

import jax
import jax.numpy as jnp
from jax import lax
from jax.experimental import pallas as pl
from jax.experimental.pallas import tpu as pltpu

F32 = jnp.float32
BF16 = jnp.bfloat16
MESH = pl.DeviceIdType.MESH

LRU_C = 8.0
EPS = 1e-6
ADAM_LR = 0.001
ADAM_B1 = 0.9
ADAM_B2 = 0.999
ADAM_EPS = 1e-08
ADAM_WD = 0.01
ADAM_STEP = 10

N_SHARDS = 4
LANES = 128
SUBLANES = 8
V7X_VMEM_BYTES = 64 * 1024 * 1024
VMEM_CAP = V7X_VMEM_BYTES * 7 // 8
GELU_K = math.sqrt(2.0 / math.pi)
GELU_C = 0.044715
FLIPS = (2, 1, 3)


def _tile(dim, pref, mult=LANES):
    t = min(pref, dim) // mult * mult
    while t >= mult:
        if dim % t == 0:
            return t
        t -= mult
    return dim


def _nbytes(shape, dtype):
    return math.prod(shape) * jnp.dtype(dtype).itemsize


def _cparams(sem, *block_bytes, scratch=0):
    est = 2 * sum(block_bytes) + scratch + (6 << 20)
    return pltpu.CompilerParams(dimension_semantics=sem, vmem_limit_bytes=min(VMEM_CAP, max(est, 16 << 20)))


def _sigmoid(x):
    return 1.0 / (1.0 + jnp.exp(-x))


def _gelu(x):
    return 0.5 * x * (1.0 + jnp.tanh(GELU_K * (x + GELU_C * x * x * x)))


def _gelu_and_grad(x):
    th = jnp.tanh(GELU_K * (x + GELU_C * x * x * x))
    g = 0.5 * x * (1.0 + th)
    dg = 0.5 * (1.0 + th) + 0.5 * x * (1.0 - th * th) * (GELU_K * (1.0 + 3.0 * GELU_C * x * x))
    return g, dg


def _softplus(x):
    z = jnp.exp(-jnp.abs(x))
    u = 1.0 + z
    l1p = jnp.where(u == 1.0, z, jnp.log(u) * z / (u - 1.0))
    return jnp.maximum(x, 0.0) + l1p


def _dot(a, b):
    return jnp.dot(a, b, preferred_element_type=F32)


def _dot_nt(a, b):
    return lax.dot_general(a, b, (((1,), (1,)), ((), ())), preferred_element_type=F32)


def _dot_tn(a, b):
    return lax.dot_general(a, b, (((0,), (0,)), ((), ())), preferred_element_type=F32)


def _row_sum(v):
    return jnp.sum(v, axis=0, keepdims=True)


def _mm_nn(a, w, l, *, epi, name, out_dtype=F32, resid=None):
    m, k = a.shape
    n = w.shape[2]
    tm, tn, tk = _tile(m, 1024, SUBLANES), _tile(n, 1024), _tile(k, 512)
    nk = k // tk

    def body(*refs):
        if epi == "resid":
            a_ref, w_ref, r_ref, o_ref, acc = refs
        elif epi == "sqrelu":
            a_ref, w_ref, o_ref, o2_ref, acc = refs
        else:
            a_ref, w_ref, o_ref, acc = refs
        kk = pl.program_id(2)

        @pl.when(kk == 0)
        def _():
            acc[...] = jnp.zeros_like(acc)

        acc[...] += _dot(a_ref[...], w_ref[...])

        @pl.when(kk == nk - 1)
        def _():
            v = acc[...]
            if epi == "resid":
                o_ref[...] = r_ref[...] + v
            elif epi == "sqrelu":
                o_ref[...] = v.astype(BF16)
                rl = jnp.maximum(v, 0.0)
                o2_ref[...] = (rl * rl).astype(BF16)
            else:
                o_ref[...] = v.astype(out_dtype)

    in_specs = [pl.BlockSpec((tm, tk), lambda i, j, kk: (i, kk)),
                pl.BlockSpec((None, tk, tn), lambda i, j, kk: (l, kk, j))]
    args = [a, w]
    o_spec = pl.BlockSpec((tm, tn), lambda i, j, kk: (i, j))
    blocks = [_nbytes((tm, tk), BF16), _nbytes((tk, tn), BF16)]
    if epi == "resid":
        in_specs.append(o_spec)
        args.append(resid)
        out_shape, out_specs = jax.ShapeDtypeStruct((m, n), F32), o_spec
        blocks += [2 * _nbytes((tm, tn), F32)]
    elif epi == "sqrelu":
        out_shape = (jax.ShapeDtypeStruct((m, n), BF16), jax.ShapeDtypeStruct((m, n), BF16))
        out_specs = (o_spec, o_spec)
        blocks += [2 * _nbytes((tm, tn), BF16)]
    else:
        out_shape, out_specs = jax.ShapeDtypeStruct((m, n), out_dtype), o_spec
        blocks += [_nbytes((tm, tn), out_dtype)]
    return pl.pallas_call(
        body, name=name, grid=(m // tm, n // tn, nk), in_specs=in_specs, out_specs=out_specs, out_shape=out_shape,
        scratch_shapes=[pltpu.VMEM((tm, tn), F32)],
        compiler_params=_cparams(("parallel", "parallel", "arbitrary"), *blocks, scratch=2 * _nbytes((tm, tn), F32)),
    )(*args)


def _mm_nt(dy, w, l, *, epi, name, out_dtype=F32, act=None):
    m, n = dy.shape
    k = w.shape[1]
    tm, tk, tr = _tile(m, 1024, SUBLANES), _tile(k, 1024), _tile(n, 512)
    nr = n // tr

    def body(*refs):
        if epi == "relu2grad":
            d_ref, w_ref, a_ref, o_ref, acc = refs
        else:
            d_ref, w_ref, o_ref, acc = refs
        rr = pl.program_id(2)

        @pl.when(rr == 0)
        def _():
            acc[...] = jnp.zeros_like(acc)

        acc[...] += _dot_nt(d_ref[...], w_ref[...])

        @pl.when(rr == nr - 1)
        def _():
            v = acc[...]
            if epi == "relu2grad":
                v = v * (2.0 * jnp.maximum(a_ref[...].astype(F32), 0.0))
            o_ref[...] = v.astype(out_dtype)

    in_specs = [pl.BlockSpec((tm, tr), lambda i, j, rr: (i, rr)),
                pl.BlockSpec((None, tk, tr), lambda i, j, rr: (l, j, rr))]
    args = [dy, w]
    o_spec = pl.BlockSpec((tm, tk), lambda i, j, rr: (i, j))
    blocks = [_nbytes((tm, tr), BF16), _nbytes((tk, tr), BF16), _nbytes((tm, tk), out_dtype)]
    if epi == "relu2grad":
        in_specs.append(o_spec)
        args.append(act)
        blocks.append(_nbytes((tm, tk), BF16))
    return pl.pallas_call(
        body, name=name, grid=(m // tm, k // tk, nr), in_specs=in_specs, out_specs=o_spec,
        out_shape=jax.ShapeDtypeStruct((m, k), out_dtype), scratch_shapes=[pltpu.VMEM((tm, tk), F32)],
        compiler_params=_cparams(("parallel", "parallel", "arbitrary"), *blocks, scratch=2 * _nbytes((tm, tk), F32)),
    )(*args)


def _mm_tn(a, dy, gbuf, l, *, name):
    m, k = a.shape
    n = dy.shape[1]
    tm, tk, tn = _tile(m, 512, SUBLANES), _tile(k, 1024), _tile(n, 1024)

    def body(a_ref, d_ref, g_in, o_ref):
        del g_in
        mm = pl.program_id(2)

        @pl.when(mm == 0)
        def _():
            o_ref[...] = jnp.zeros_like(o_ref)

        o_ref[...] += _dot_tn(a_ref[...], d_ref[...])

    blocks = [_nbytes((tm, tk), BF16), _nbytes((tm, tn), BF16), _nbytes((tk, tn), F32)]
    return pl.pallas_call(
        body, name=name, grid=(k // tk, n // tn, m // tm),
        in_specs=[pl.BlockSpec((tm, tk), lambda i, j, mm: (mm, i)), pl.BlockSpec((tm, tn), lambda i, j, mm: (mm, j)),
                  pl.BlockSpec(memory_space=pl.ANY)],
        out_specs=pl.BlockSpec((None, tk, tn), lambda i, j, mm: (l, i, j)),
        out_shape=jax.ShapeDtypeStruct(gbuf.shape, F32), input_output_aliases={2: 0},
        compiler_params=_cparams(("parallel", "parallel", "arbitrary"), *blocks, scratch=_nbytes((tk, tn), F32)),
    )(a, dy, gbuf)


def _norm_fwd(x, g, *, name):
    t, d = x.shape
    tt = _tile(t, 512, SUBLANES)

    def body(x_ref, g_ref, o_ref):
        xv = x_ref[...]
        r = lax.rsqrt(jnp.mean(xv * xv, axis=-1, keepdims=True) + EPS)
        o_ref[...] = ((xv * r) * g_ref[...]).astype(BF16)

    blk = pl.BlockSpec((tt, d), lambda i: (i, 0))
    return pl.pallas_call(
        body, name=name, grid=(t // tt,), in_specs=[blk, pl.BlockSpec((1, d), lambda i: (0, 0))], out_specs=blk,
        out_shape=jax.ShapeDtypeStruct((t, d), BF16),
        compiler_params=_cparams(("parallel",), 4 * _nbytes((tt, d), F32)),
    )(x, g)


def _norm_bwd(dh, x, g, dres, *, name):
    t, d = x.shape
    tt = _tile(t, 512, SUBLANES)

    def body(dh_ref, x_ref, g_ref, dr_ref, dx_ref, dxb_ref, dg_ref):
        @pl.when(pl.program_id(0) == 0)
        def _():
            dg_ref[...] = jnp.zeros_like(dg_ref)

        xv = x_ref[...]
        dhv = dh_ref[...]
        r = lax.rsqrt(jnp.mean(xv * xv, axis=-1, keepdims=True) + EPS)
        xh = xv * r
        dhg = dhv * g_ref[...]
        dx = dr_ref[...] + r * (dhg - xh * jnp.mean(dhg * xh, axis=-1, keepdims=True))
        dx_ref[...] = dx
        dxb_ref[...] = dx.astype(BF16)
        dg_ref[...] += _row_sum(dhv * xh)

    blk = pl.BlockSpec((tt, d), lambda i: (i, 0))
    row = pl.BlockSpec((1, d), lambda i: (0, 0))
    return pl.pallas_call(
        body, name=name, grid=(t // tt,), in_specs=[blk, blk, row, blk], out_specs=(blk, blk, row),
        out_shape=(jax.ShapeDtypeStruct((t, d), F32), jax.ShapeDtypeStruct((t, d), BF16), jax.ShapeDtypeStruct((1, d), F32)),
        compiler_params=_cparams(("arbitrary",), 8 * _nbytes((tt, d), F32)),
    )(dh, x, g, dres)


def _loss_and_grad(x, g, target, *, name):
    t, d = x.shape
    tt = _tile(t, 512, SUBLANES)
    nt = t // tt

    def body(x_ref, g_ref, t_ref, loss_ref, dx_ref, dxb_ref, dg_ref, acc):
        i = pl.program_id(0)

        @pl.when(i == 0)
        def _():
            dg_ref[...] = jnp.zeros_like(dg_ref)
            acc[...] = jnp.zeros_like(acc)

        xv = x_ref[...]
        gv = g_ref[...]
        r = lax.rsqrt(jnp.mean(xv * xv, axis=-1, keepdims=True) + EPS)
        xh = xv * r
        err = xh * gv - t_ref[...]
        acc[...] += _row_sum(err * err)
        dy = err * (1.0 / d)
        dyg = dy * gv
        dx = r * (dyg - xh * jnp.mean(dyg * xh, axis=-1, keepdims=True))
        dx_ref[...] = dx
        dxb_ref[...] = dx.astype(BF16)
        dg_ref[...] += _row_sum(dy * xh)

        @pl.when(i == nt - 1)
        def _():
            loss_ref[...] = jnp.full(loss_ref.shape, (0.5 / d) * jnp.sum(acc[...]), F32)

    blk = pl.BlockSpec((tt, d), lambda i: (i, 0))
    row = pl.BlockSpec((1, d), lambda i: (0, 0))
    return pl.pallas_call(
        body, name=name, grid=(nt,), in_specs=[blk, row, blk],
        out_specs=(pl.BlockSpec((1, LANES), lambda i: (0, 0)), blk, blk, row),
        out_shape=(jax.ShapeDtypeStruct((1, LANES), F32), jax.ShapeDtypeStruct((t, d), F32),
                   jax.ShapeDtypeStruct((t, d), BF16), jax.ShapeDtypeStruct((1, d), F32)),
        scratch_shapes=[pltpu.VMEM((1, d), F32)],
        compiler_params=_cparams(("arbitrary",), 8 * _nbytes((tt, d), F32)),
    )(x, g, target)


def _scan_fwd(a, u, tt, row):
    s = 1
    while s < tt:
        a_s = pltpu.roll(a, s, 0)
        u_s = pltpu.roll(u, s, 0)
        m = row >= s
        u = jnp.where(m, a * u_s + u, u)
        a = jnp.where(m, a * a_s, a)
        s *= 2
    return a, u


def _scan_bwd(a, u, tt, row):
    s = 1
    while s < tt:
        a_s = pltpu.roll(a, tt - s, 0)
        u_s = pltpu.roll(u, tt - s, 0)
        m = row < tt - s
        u = jnp.where(m, a * u_s + u, u)
        a = jnp.where(m, a * a_s, a)
        s *= 2
    return u


def _a_gates(xc, wa, wx, ba, bx, lam):
    xcb = xc.astype(BF16)
    ra = _sigmoid(_dot(xcb, wa) + ba)
    ia = _sigmoid(_dot(xcb, wx) + bx)
    sp = _softplus(-lam)
    la = (-LRU_C) * ra * sp
    a = jnp.exp(la)
    mult = jnp.sqrt(-jnp.tanh(la) * (a * a + 1.0))
    return xcb, ra, ia, sp, a, mult


def _a_core_fwd(p, cw, vec, wa, wx, *, name):
    t, r2 = p.shape
    r = r2 // 2
    ng, gw = wa.shape[0], wa.shape[1]
    kc = cw.shape[0]
    tt = _tile(t, 128, SUBLANES)
    nt = t // tt

    def body(p_ref, cw_ref, vec_ref, wa_ref, wx_ref, y_ref, h_ref, ext, hcar):
        i = pl.program_id(0)

        @pl.when(i == 0)
        def _():
            ext[...] = jnp.zeros_like(ext)
            hcar[...] = jnp.zeros_like(hcar)

        row = lax.broadcasted_iota(jnp.int32, (tt, gw), 0)
        for q in range(ng):
            cs = slice(q * gw, (q + 1) * gw)
            gate = p_ref[:, q * gw:(q + 1) * gw]
            xr = p_ref[:, r + q * gw:r + (q + 1) * gw]
            ext[q, SUBLANES:, :] = xr
            xc = vec_ref[0:1, cs]
            for k in range(kc):
                xc = xc + cw_ref[k:k + 1, cs] * ext[q, pl.ds(SUBLANES - (kc - 1) + k, tt), :]
            ext[q, 0:SUBLANES, :] = xr[tt - SUBLANES:, :]
            _, _, ia, _, a, mult = _a_gates(xc, wa_ref[q], wx_ref[q], vec_ref[1:2, cs], vec_ref[2:3, cs], vec_ref[3:4, cs])
            acum, hloc = _scan_fwd(a, mult * (ia * xc), tt, row)
            h = hloc + acum * hcar[0:1, cs]
            hcar[0:1, cs] = _row_sum(jnp.where(row == tt - 1, h, 0.0))
            h_ref[:, cs] = h
            y_ref[:, cs] = (h * _gelu(gate)).astype(BF16)

    blocks = [_nbytes((tt, r2), F32), _nbytes((tt, r), F32), _nbytes((tt, r), BF16), 2 * _nbytes((ng, gw, gw), BF16)]
    return pl.pallas_call(
        body, name=name, grid=(nt,),
        in_specs=[pl.BlockSpec((tt, r2), lambda i: (i, 0)), pl.BlockSpec((kc, r), lambda i: (0, 0)),
                  pl.BlockSpec((SUBLANES, r), lambda i: (0, 0)),
                  pl.BlockSpec((ng, gw, gw), lambda i: (0, 0, 0)), pl.BlockSpec((ng, gw, gw), lambda i: (0, 0, 0))],
        out_specs=(pl.BlockSpec((tt, r), lambda i: (i, 0)), pl.BlockSpec((tt, r), lambda i: (i, 0))),
        out_shape=(jax.ShapeDtypeStruct((t, r), BF16), jax.ShapeDtypeStruct((t, r), F32)),
        scratch_shapes=[pltpu.VMEM((ng, SUBLANES + tt, gw), F32), pltpu.VMEM((SUBLANES, r), F32)],
        compiler_params=_cparams(("arbitrary",), *blocks, scratch=24 * _nbytes((tt, gw), F32)),
    )(p, cw, vec, wa, wx)


def _a_core_bwd(p, hs, dy, cw, vec, wa, wx, *, name):
    t, r2 = p.shape
    r = r2 // 2
    ng, gw = wa.shape[0], wa.shape[1]
    kc = cw.shape[0]
    tt = _tile(t, 128, SUBLANES)
    nt = t // tt
    hb = tt // SUBLANES
    r_cb, r_ba, r_bx, r_lam = kc, kc + 1, kc + 2, kc + 3

    def body(p_ref, ph_ref, h_ref, hh_ref, dy_ref, cw_ref, vec_ref, wa_ref, wx_ref,
             dp_ref, sm_ref, dwa_ref, dwx_ref, ext, hext, dext, cin):
        i = pl.program_id(0)
        first = i == nt - 1
        last = i == 0

        @pl.when(last)
        def _():
            sm_ref[...] = jnp.zeros_like(sm_ref)
            dwa_ref[...] = jnp.zeros_like(dwa_ref)
            dwx_ref[...] = jnp.zeros_like(dwx_ref)
            dext[...] = jnp.zeros_like(dext)
            cin[...] = jnp.zeros_like(cin)

        row = lax.broadcasted_iota(jnp.int32, (tt, gw), 0)
        keep = jnp.where(first, 0.0, 1.0)
        for q in range(ng):
            cs = slice(q * gw, (q + 1) * gw)
            gate = p_ref[:, q * gw:(q + 1) * gw]
            xr = p_ref[:, r + q * gw:r + (q + 1) * gw]
            ext[0:SUBLANES, :] = ph_ref[:, r + q * gw:r + (q + 1) * gw] * keep
            ext[SUBLANES:, :] = xr
            xc = vec_ref[0:1, cs]
            for k in range(kc):
                xc = xc + cw_ref[k:k + 1, cs] * ext[pl.ds(SUBLANES - (kc - 1) + k, tt), :]
            lam = vec_ref[3:4, cs]
            xcb, ra, ia, sp, a, mult = _a_gates(xc, wa_ref[q], wx_ref[q], vec_ref[1:2, cs], vec_ref[2:3, cs], lam)
            hs = h_ref[:, cs]
            hext[0:SUBLANES, :] = hh_ref[:, cs] * keep
            hext[SUBLANES:, :] = hs
            hprev = hext[pl.ds(SUBLANES - 1, tt), :]
            dyv = dy_ref[:, cs]
            gl, dgl = _gelu_and_grad(gate)
            b0 = dyv * gl + jnp.where(row == tt - 1, cin[0:1, cs], 0.0)
            dh = _scan_bwd(pltpu.roll(a, tt - 1, 0), b0, tt, row)
            cin[0:1, cs] = _row_sum(jnp.where(row == 0, a * dh, 0.0))
            da = dh * hprev
            dmult = dh * (ia * xc)
            dia = dh * (mult * xc)
            dxc = dh * (mult * ia)
            dla = da * a - dmult * (a * a) / mult
            dra = dla * ((-LRU_C) * sp)
            sm_ref[r_lam:r_lam + 1, cs] += _row_sum(dla * ((-LRU_C) * ra))
            dpa = dra * ra * (1.0 - ra)
            dpx = dia * ia * (1.0 - ia)
            sm_ref[r_ba:r_ba + 1, cs] += _row_sum(dpa)
            sm_ref[r_bx:r_bx + 1, cs] += _row_sum(dpx)
            dpab = dpa.astype(BF16)
            dpxb = dpx.astype(BF16)
            dxc = dxc + _dot_nt(dpab, wa_ref[q]) + _dot_nt(dpxb, wx_ref[q])
            dwa_ref[q] += _dot_tn(xcb, dpab)
            dwx_ref[q] += _dot_tn(xcb, dpxb)
            sm_ref[r_cb:r_cb + 1, cs] += _row_sum(dxc)
            for k in range(kc):
                sm_ref[k:k + 1, cs] += _row_sum(dxc * ext[pl.ds(SUBLANES - (kc - 1) + k, tt), :])
            dext[q, 0:tt, :] = dxc
            dxr = jnp.zeros((tt, gw), F32)
            for k in range(kc):
                dxr = dxr + cw_ref[k:k + 1, cs] * dext[q, pl.ds(kc - 1 - k, tt), :]
            dext[q, tt:, :] = dxc[0:SUBLANES, :]
            dp_ref[:, q * gw:(q + 1) * gw] = (dyv * hs * dgl).astype(BF16)
            dp_ref[:, r + q * gw:r + (q + 1) * gw] = dxr.astype(BF16)

        @pl.when(first)
        def _():
            lamv = vec_ref[3:4, :]
            sm_ref[r_lam:r_lam + 1, :] = sm_ref[r_lam:r_lam + 1, :] * (-_sigmoid(-lamv))

    def tile_idx(i):
        return (nt - 1 - i, 0)

    def halo_idx(i):
        return (jnp.maximum((nt - 1 - i) * hb - 1, 0), 0)

    const2 = lambda i: (0, 0)
    const3 = lambda i: (0, 0, 0)
    blocks = [_nbytes((tt, r2), F32), 2 * _nbytes((tt, r), F32), _nbytes((tt, r2), BF16), 4 * _nbytes((ng, gw, gw), F32)]
    return pl.pallas_call(
        body, name=name, grid=(nt,),
        in_specs=[pl.BlockSpec((tt, r2), tile_idx), pl.BlockSpec((SUBLANES, r2), halo_idx),
                  pl.BlockSpec((tt, r), tile_idx), pl.BlockSpec((SUBLANES, r), halo_idx),
                  pl.BlockSpec((tt, r), tile_idx),
                  pl.BlockSpec((kc, r), const2), pl.BlockSpec((SUBLANES, r), const2),
                  pl.BlockSpec((ng, gw, gw), const3), pl.BlockSpec((ng, gw, gw), const3)],
        out_specs=(pl.BlockSpec((tt, r2), tile_idx), pl.BlockSpec((2 * SUBLANES, r), const2),
                   pl.BlockSpec((ng, gw, gw), const3), pl.BlockSpec((ng, gw, gw), const3)),
        out_shape=(jax.ShapeDtypeStruct((t, r2), BF16), jax.ShapeDtypeStruct((2 * SUBLANES, r), F32),
                   jax.ShapeDtypeStruct((ng, gw, gw), F32), jax.ShapeDtypeStruct((ng, gw, gw), F32)),
        scratch_shapes=[pltpu.VMEM((SUBLANES + tt, gw), F32), pltpu.VMEM((SUBLANES + tt, gw), F32),
                        pltpu.VMEM((ng, tt + SUBLANES, gw), F32), pltpu.VMEM((SUBLANES, r), F32)],
        compiler_params=_cparams(("arbitrary",), *blocks, scratch=40 * _nbytes((tt, gw), F32)),
    )(p, p, hs, hs, dy, cw, vec, wa, wx)


def _b_mixed(vb, wc_ref, mix_ref, tt, ch, ngr, gd):
    for n in range(tt // ch):
        for g in range(ngr):
            mix_ref[n * ch:(n + 1) * ch, g * gd:(g + 1) * gd] = _dot(wc_ref[g], vb[n * ch:(n + 1) * ch, g * gd:(g + 1) * gd])


def _b_core_fwd(p, ng_row, wc, bias_full, *, name):
    t, s2 = p.shape
    s = s2 // 2
    ngr, ch = wc.shape[0], wc.shape[1]
    gd = s // ngr
    tt = _tile(t, 2 * ch, ch)

    def body(p_ref, ng_ref, wc_ref, b_ref, y_ref, mix):
        z = _gelu(p_ref[...])
        u = z[:, :s]
        v = z[:, s:]
        rs = lax.rsqrt(jnp.mean(v * v, axis=-1, keepdims=True) + EPS)
        vb = ((v * rs) * ng_ref[...]).astype(BF16)
        _b_mixed(vb, wc_ref, mix, tt, ch, ngr, gd)
        for n in range(tt // ch):
            rows = slice(n * ch, (n + 1) * ch)
            y_ref[rows, :] = (u[rows, :] * (mix[rows, :] + b_ref[...])).astype(BF16)

    blocks = [_nbytes((tt, s2), F32), _nbytes((tt, s), BF16), _nbytes((ngr, ch, ch), BF16), _nbytes((ch, s), F32)]
    return pl.pallas_call(
        body, name=name, grid=(t // tt,),
        in_specs=[pl.BlockSpec((tt, s2), lambda i: (i, 0)), pl.BlockSpec((1, s), lambda i: (0, 0)),
                  pl.BlockSpec((ngr, ch, ch), lambda i: (0, 0, 0)), pl.BlockSpec((ch, s), lambda i: (0, 0))],
        out_specs=pl.BlockSpec((tt, s), lambda i: (i, 0)), out_shape=jax.ShapeDtypeStruct((t, s), BF16),
        scratch_shapes=[pltpu.VMEM((tt, s), F32)],
        compiler_params=_cparams(("parallel",), *blocks, scratch=12 * _nbytes((tt, s), F32)),
    )(p, ng_row, wc, bias_full)


def _b_core_bwd(p, dy, ng_row, wc, wct, bias_full, *, name):
    t, s2 = p.shape
    s = s2 // 2
    ngr, ch = wc.shape[0], wc.shape[1]
    gd = s // ngr
    tt = _tile(t, 2 * ch, ch)

    def body(p_ref, dy_ref, ng_ref, wc_ref, wct_ref, b_ref, dp_ref, dng_ref, dwc_ref, db_ref, mix, dvn):
        @pl.when(pl.program_id(0) == 0)
        def _():
            dng_ref[...] = jnp.zeros_like(dng_ref)
            dwc_ref[...] = jnp.zeros_like(dwc_ref)
            db_ref[...] = jnp.zeros_like(db_ref)

        pv = p_ref[...]
        z, dz = _gelu_and_grad(pv)
        u = z[:, :s]
        v = z[:, s:]
        rs = lax.rsqrt(jnp.mean(v * v, axis=-1, keepdims=True) + EPS)
        vh = v * rs
        ngv = ng_ref[...]
        vb = (vh * ngv).astype(BF16)
        _b_mixed(vb, wc_ref, mix, tt, ch, ngr, gd)
        dyv = dy_ref[...]
        dmx = dyv * u
        dmb = dmx.astype(BF16)
        for n in range(tt // ch):
            rows = slice(n * ch, (n + 1) * ch)
            mix[rows, :] = mix[rows, :] + b_ref[...]
            db_ref[...] += dmx[rows, :]
            for g in range(ngr):
                cols = slice(g * gd, (g + 1) * gd)
                dvn[rows, cols] = _dot(wct_ref[g], dmb[rows, cols])
                dwc_ref[g] += _dot_nt(dmb[rows, cols], vb[rows, cols])
        du = dyv * mix[...]
        dvnv = dvn[...]
        dng_ref[...] += _row_sum(dvnv * vh)
        dvh = dvnv * ngv
        dv = rs * (dvh - vh * jnp.mean(dvh * vh, axis=-1, keepdims=True))
        dp_ref[:, :s] = (du * dz[:, :s]).astype(BF16)
        dp_ref[:, s:] = (dv * dz[:, s:]).astype(BF16)

    const2 = lambda i: (0, 0)
    const3 = lambda i: (0, 0, 0)
    blocks = [_nbytes((tt, s2), F32), _nbytes((tt, s), F32), _nbytes((tt, s2), BF16),
              2 * _nbytes((ngr, ch, ch), F32), 2 * _nbytes((ch, s), F32)]
    return pl.pallas_call(
        body, name=name, grid=(t // tt,),
        in_specs=[pl.BlockSpec((tt, s2), lambda i: (i, 0)), pl.BlockSpec((tt, s), lambda i: (i, 0)),
                  pl.BlockSpec((1, s), const2), pl.BlockSpec((ngr, ch, ch), const3), pl.BlockSpec((ngr, ch, ch), const3),
                  pl.BlockSpec((ch, s), const2)],
        out_specs=(pl.BlockSpec((tt, s2), lambda i: (i, 0)), pl.BlockSpec((1, s), const2),
                   pl.BlockSpec((ngr, ch, ch), const3), pl.BlockSpec((ch, s), const2)),
        out_shape=(jax.ShapeDtypeStruct((t, s2), BF16), jax.ShapeDtypeStruct((1, s), F32),
                   jax.ShapeDtypeStruct((ngr, ch, ch), F32), jax.ShapeDtypeStruct((ch, s), F32)),
        scratch_shapes=[pltpu.VMEM((tt, s), F32), pltpu.VMEM((tt, s), F32)],
        compiler_params=_cparams(("arbitrary",), *blocks, scratch=20 * _nbytes((tt, s), F32)),
    )(p, dy, ng_row, wc, wct, bias_full)


def _c_core_fwd(p, cw, *, name):
    t, c3 = p.shape
    c = c3 // 3
    kc = cw.shape[0]
    tt = _tile(t, 256, SUBLANES)
    hb = tt // SUBLANES

    def body(p_ref, ph_ref, cw_ref, y_ref, ext):
        keep = jnp.where(pl.program_id(0) == 0, 0.0, 1.0)
        ext[0:SUBLANES, :] = ph_ref[:, c:2 * c] * ph_ref[:, 2 * c:] * keep
        ext[SUBLANES:, :] = p_ref[:, c:2 * c] * p_ref[:, 2 * c:]
        cq = jnp.zeros((tt, c), F32)
        for k in range(kc):
            cq = cq + cw_ref[k:k + 1, :] * ext[pl.ds(SUBLANES - (kc - 1) + k, tt), :]
        y_ref[...] = (p_ref[:, :c] * cq).astype(BF16)

    blocks = [_nbytes((tt, c3), F32), _nbytes((tt, c), BF16)]
    return pl.pallas_call(
        body, name=name, grid=(t // tt,),
        in_specs=[pl.BlockSpec((tt, c3), lambda i: (i, 0)),
                  pl.BlockSpec((SUBLANES, c3), lambda i: (jnp.maximum(i * hb - 1, 0), 0)),
                  pl.BlockSpec((kc, c), lambda i: (0, 0))],
        out_specs=pl.BlockSpec((tt, c), lambda i: (i, 0)), out_shape=jax.ShapeDtypeStruct((t, c), BF16),
        scratch_shapes=[pltpu.VMEM((SUBLANES + tt, c), F32)],
        compiler_params=_cparams(("parallel",), *blocks, scratch=8 * _nbytes((tt, c), F32)),
    )(p, p, cw)


def _c_core_bwd(p, dy, cw, *, name):
    t, c3 = p.shape
    c = c3 // 3
    kc = cw.shape[0]
    tt = _tile(t, 256, SUBLANES)
    hb = tt // SUBLANES
    nt = t // tt

    def body(p_ref, ph_ref, pn_ref, dy_ref, dyn_ref, cw_ref, dp_ref, dw_ref, ext, dext):
        i = pl.program_id(0)

        @pl.when(i == 0)
        def _():
            dw_ref[...] = jnp.zeros_like(dw_ref)

        keep_prev = jnp.where(i == 0, 0.0, 1.0)
        keep_next = jnp.where(i == nt - 1, 0.0, 1.0)
        gb = p_ref[:, :c]
        gc = p_ref[:, c:2 * c]
        xv = p_ref[:, 2 * c:]
        ext[0:SUBLANES, :] = ph_ref[:, c:2 * c] * ph_ref[:, 2 * c:] * keep_prev
        ext[SUBLANES:, :] = gc * xv
        dyv = dy_ref[...]
        dcq = dyv * gb
        dext[0:tt, :] = dcq
        dext[tt:, :] = dyn_ref[...] * pn_ref[:, :c] * keep_next
        cq = jnp.zeros((tt, c), F32)
        dq = jnp.zeros((tt, c), F32)
        for k in range(kc):
            tap = ext[pl.ds(SUBLANES - (kc - 1) + k, tt), :]
            cq = cq + cw_ref[k:k + 1, :] * tap
            dw_ref[k:k + 1, :] += _row_sum(dcq * tap)
            dq = dq + cw_ref[k:k + 1, :] * dext[pl.ds(kc - 1 - k, tt), :]
        dp_ref[:, :c] = (dyv * cq).astype(BF16)
        dp_ref[:, c:2 * c] = (dq * xv).astype(BF16)
        dp_ref[:, 2 * c:] = (dq * gc).astype(BF16)

    prev_idx = lambda i: (jnp.maximum(i * hb - 1, 0), 0)
    next_idx = lambda i: (jnp.minimum((i + 1) * hb, t // SUBLANES - 1), 0)
    blocks = [_nbytes((tt, c3), F32), _nbytes((tt, c), F32), _nbytes((tt, c3), BF16)]
    return pl.pallas_call(
        body, name=name, grid=(nt,),
        in_specs=[pl.BlockSpec((tt, c3), lambda i: (i, 0)), pl.BlockSpec((SUBLANES, c3), prev_idx),
                  pl.BlockSpec((SUBLANES, c3), next_idx), pl.BlockSpec((tt, c), lambda i: (i, 0)),
                  pl.BlockSpec((SUBLANES, c), next_idx), pl.BlockSpec((kc, c), lambda i: (0, 0))],
        out_specs=(pl.BlockSpec((tt, c3), lambda i: (i, 0)), pl.BlockSpec((SUBLANES, c), lambda i: (0, 0))),
        out_shape=(jax.ShapeDtypeStruct((t, c3), BF16), jax.ShapeDtypeStruct((SUBLANES, c), F32)),
        scratch_shapes=[pltpu.VMEM((SUBLANES + tt, c), F32), pltpu.VMEM((tt + SUBLANES, c), F32)],
        compiler_params=_cparams(("arbitrary",), *blocks, scratch=12 * _nbytes((tt, c), F32)),
    )(p, p, p, dy, dy, cw)


def _adam(g, w, m, v, *, name):
    shape = w.shape
    cols = shape[-1]
    rows = math.prod(shape[:-1]) if len(shape) > 1 else 1
    g2, w2, m2, v2 = (z.reshape(rows, cols) for z in (g, w, m, v))
    tr = _tile(rows, 256, SUBLANES)
    c1 = 1.0 / (1.0 - ADAM_B1 ** ADAM_STEP)
    c2 = 1.0 / (1.0 - ADAM_B2 ** ADAM_STEP)

    def body(g_ref, w_ref, m_ref, v_ref, d_ref, nm_ref, nv_ref):
        gv = g_ref[...]
        nm = ADAM_B1 * m_ref[...] + (1.0 - ADAM_B1) * gv
        nv = ADAM_B2 * v_ref[...] + (1.0 - ADAM_B2) * (gv * gv)
        d_ref[...] = -ADAM_LR * ((nm * c1) / (jnp.sqrt(nv * c2) + ADAM_EPS) + ADAM_WD * w_ref[...])
        nm_ref[...] = nm
        nv_ref[...] = nv

    blk = pl.BlockSpec((tr, cols), lambda i: (i, 0))
    sds = jax.ShapeDtypeStruct((rows, cols), F32)
    pad = _nbytes((tr, -(-cols // LANES) * LANES), F32)
    outs = pl.pallas_call(
        body, name=name, grid=(rows // tr,), in_specs=[blk] * 4, out_specs=(blk,) * 3, out_shape=(sds,) * 3,
        compiler_params=_cparams(("parallel",), 7 * pad),
    )(g2, w2, m2, v2)
    return tuple(o.reshape(shape) for o in outs)


def _coords():
    return lax.axis_index("x"), lax.axis_index("y"), lax.axis_index("c")


def _chip_peers(x, y):
    return ((1 - x, y), (x, 1 - y), (1 - x, 1 - y))


def _for_shard(s, fn):
    for j in range(N_SHARDS):
        pl.when(s == j)(functools.partial(fn, j))


def _win(ref, kind, j, h, cs):
    if kind == "c":
        return ref.at[:, h, :, pl.ds(j * cs, cs)]
    return ref.at[:, j, h]


def _gather_weights(shards, kinds):
    n = len(shards)
    views, full_shapes = [], []
    for a, kind in zip(shards, kinds):
        l, r, cs = a.shape
        views.append(a.reshape(l, 2, r // 2, cs))
        full_shapes.append((l, 2, r // 2, N_SHARDS * cs) if kind == "c" else (l, N_SHARDS, 2, r // 2, cs))

    def body(*refs):
        sh, fu = refs[:n], refs[n:2 * n]
        lsem, ssem, rsem, fsem, gsem = refs[2 * n:]
        x, y, c = _coords()
        s = 2 * x + y
        peers = _chip_peers(x, y)
        sib = (x, y, 1 - c)

        def icopy(a, j, k):
            cs = shards[a].shape[2]
            dst = _win(fu[a], kinds[a], j, c, cs)
            return pltpu.make_async_remote_copy(src_ref=sh[a].at[:, c], dst_ref=dst, send_sem=ssem.at[a, k],
                                                recv_sem=rsem.at[a, k], device_id=(*peers[k], c), device_id_type=MESH)

        def fcopy(a, j, k, h):
            cs = shards[a].shape[2]
            w = _win(fu[a], kinds[a], j, h, cs)
            return pltpu.make_async_remote_copy(src_ref=w, dst_ref=w, send_sem=fsem.at[a, k], recv_sem=gsem.at[a, k],
                                                device_id=sib, device_id_type=MESH)

        def lcopy(a, j):
            cs = shards[a].shape[2]
            dst = fu[a].at[:, :, :, pl.ds(j * cs, cs)] if kinds[a] == "c" else fu[a].at[:, j]
            return pltpu.make_async_copy(sh[a], dst, lsem.at[a])

        def run(j):
            for a in range(n):
                lcopy(a, j).start()
                for k in range(3):
                    icopy(a, j, k).start()
            for k in range(3):
                for a in range(n):
                    icopy(a, j ^ FLIPS[k], k).wait_recv()
                    fcopy(a, j ^ FLIPS[k], k, c).start()
            for k in range(3):
                for a in range(n):
                    fcopy(a, j ^ FLIPS[k], k, 1 - c).wait_recv()
            for a in range(n):
                for k in range(3):
                    icopy(a, j, k).wait_send()
                    fcopy(a, j ^ FLIPS[k], k, c).wait_send()
                lcopy(a, j).wait()

        _for_shard(s, run)

    any_spec = pl.BlockSpec(memory_space=pl.ANY)
    outs = pl.pallas_call(
        body, name="gather_weights", in_specs=[any_spec] * n, out_specs=[any_spec] * n,
        out_shape=[jax.ShapeDtypeStruct(fs, BF16) for fs in full_shapes],
        scratch_shapes=[pltpu.SemaphoreType.DMA((n,))] + [pltpu.SemaphoreType.DMA((n, 3))] * 4,
    )(*views)
    full = []
    for o, a, kind in zip(outs, shards, kinds):
        l, r, cs = a.shape
        full.append(o.reshape(l, r, N_SHARDS * cs) if kind == "c" else o.reshape(l, N_SHARDS * r, cs))
    return full


def _pair_exchange(grads, kinds):
    n = len(grads)
    views, half_shapes = [], []
    for g, kind in zip(grads, kinds):
        l, rr, cc = g.shape
        if kind == "c":
            views.append(g.reshape(l, 2, rr // 2, cc))
            half_shapes.append((l, rr // 2, cc))
        else:
            ks = rr // N_SHARDS
            views.append(g.reshape(l, N_SHARDS, 2, ks // 2, cc))
            half_shapes.append((l, N_SHARDS, ks // 2, cc))

    def body(*refs):
        gv, pr = refs[:n], refs[n:2 * n]
        ssem, rsem = refs[2 * n:]
        x, y, c = _coords()

        def copy(a):
            src = gv[a].at[:, 1 - c] if kinds[a] == "c" else gv[a].at[:, :, 1 - c]
            return pltpu.make_async_remote_copy(src_ref=src, dst_ref=pr[a], send_sem=ssem.at[a], recv_sem=rsem.at[a],
                                                device_id=(x, y, 1 - c), device_id_type=MESH)

        for a in range(n):
            copy(a).start()
        for a in range(n):
            copy(a).wait()

    any_spec = pl.BlockSpec(memory_space=pl.ANY)
    return pl.pallas_call(
        body, name="grad_pair_exchange", in_specs=[any_spec] * n, out_specs=[any_spec] * n,
        out_shape=[jax.ShapeDtypeStruct(hs, F32) for hs in half_shapes],
        scratch_shapes=[pltpu.SemaphoreType.DMA((n,)), pltpu.SemaphoreType.DMA((n,))],
    )(*views)


def _pair_sum(g, pair, kind, sc, *, name):
    l, rr, cc = g.shape
    if kind == "c":
        o, hr = l, rr // 2
    else:
        o, hr = l * N_SHARDS, rr // N_SHARDS // 2
    g4 = g.reshape(o, 2, hr, cc)
    p3 = pair.reshape(o, hr, cc)
    tr, tc = _tile(hr, 512, 16), _tile(cc, 1024)

    def body(sc_ref, g_ref, p_ref, o_ref):
        del sc_ref
        o_ref[...] = (g_ref[...] + p_ref[...]).astype(BF16)

    blk = pl.BlockSpec((None, tr, tc), lambda ob, i, j, sc_ref: (ob, i, j))
    out = pl.pallas_call(
        body, name=name,
        grid_spec=pltpu.PrefetchScalarGridSpec(
            num_scalar_prefetch=1, grid=(o, hr // tr, cc // tc),
            in_specs=[pl.BlockSpec((None, None, tr, tc), lambda ob, i, j, sc_ref: (ob, sc_ref[1], i, j)), blk],
            out_specs=blk),
        out_shape=jax.ShapeDtypeStruct((o, hr, cc), BF16),
        compiler_params=_cparams(("parallel", "parallel", "parallel"), 3 * _nbytes((tr, tc), F32)),
    )(sc, g4, p3)
    return out.reshape(pair.shape)


def _chip_exchange(csums, kinds):
    n = len(csums)
    views, piece_shapes = [], []
    for cs_arr, kind in zip(csums, kinds):
        if kind == "c":
            l, hr, cc = cs_arr.shape
            views.append(cs_arr)
            piece_shapes.append((l, hr, cc // N_SHARDS))
        else:
            l, _, hr, cc = cs_arr.shape
            views.append(cs_arr)
            piece_shapes.append((l, hr, cc))

    def body(*refs):
        cv, rc = refs[:n], refs[n:2 * n]
        ssem, rsem = refs[2 * n:]
        x, y, c = _coords()
        s = 2 * x + y
        peers = _chip_peers(x, y)

        def copy(a, j, k):
            if kinds[a] == "c":
                w = piece_shapes[a][2]
                src = cv[a].at[:, :, pl.ds(j * w, w)]
            else:
                src = cv[a].at[:, j]
            return pltpu.make_async_remote_copy(src_ref=src, dst_ref=rc[a].at[k], send_sem=ssem.at[a, k],
                                                recv_sem=rsem.at[a, k], device_id=(*peers[k], c), device_id_type=MESH)

        def run(j):
            for a in range(n):
                for k in range(3):
                    copy(a, j ^ FLIPS[k], k).start()
            for a in range(n):
                for k in range(3):
                    copy(a, j ^ FLIPS[k], k).wait()

        _for_shard(s, run)

    any_spec = pl.BlockSpec(memory_space=pl.ANY)
    return pl.pallas_call(
        body, name="grad_chip_exchange", in_specs=[any_spec] * n, out_specs=[any_spec] * n,
        out_shape=[jax.ShapeDtypeStruct((3, *ps), BF16) for ps in piece_shapes],
        scratch_shapes=[pltpu.SemaphoreType.DMA((n, 3)), pltpu.SemaphoreType.DMA((n, 3))],
    )(*views)


def _final_sum(g, pair, recv, kind, sc, *, name):
    l, rr, cc = g.shape
    if kind == "c":
        hr, w = rr // 2, cc // N_SHARDS
        g_v = g.reshape(l, 2, hr, cc)
        p_v = pair
        tr, tc = _tile(hr, 512, 16), _tile(w, 1024)
        nb = w // tc
        g_spec = pl.BlockSpec((None, None, tr, tc), lambda lb, i, j, sc_ref: (lb, sc_ref[1], i, sc_ref[0] * nb + j))
        p_spec = pl.BlockSpec((None, tr, tc), lambda lb, i, j, sc_ref: (lb, i, sc_ref[0] * nb + j))
    else:
        ks = rr // N_SHARDS
        hr, w = ks // 2, cc
        g_v = g.reshape(l, N_SHARDS, 2, hr, cc)
        p_v = pair
        tr, tc = _tile(hr, 512, 16), _tile(w, 1024)
        g_spec = pl.BlockSpec((None, None, None, tr, tc), lambda lb, i, j, sc_ref: (lb, sc_ref[0], sc_ref[1], i, j))
        p_spec = pl.BlockSpec((None, None, tr, tc), lambda lb, i, j, sc_ref: (lb, sc_ref[0], i, j))

    def body(sc_ref, g_ref, p_ref, r_ref, o_ref):
        del sc_ref
        own = (g_ref[...] + p_ref[...]) + r_ref[1].astype(F32)
        o_ref[...] = own + (r_ref[0].astype(F32) + r_ref[2].astype(F32))

    o_spec = pl.BlockSpec((None, tr, tc), lambda lb, i, j, sc_ref: (lb, i, j))
    return pl.pallas_call(
        body, name=name,
        grid_spec=pltpu.PrefetchScalarGridSpec(
            num_scalar_prefetch=1, grid=(l, hr // tr, w // tc),
            in_specs=[g_spec, p_spec, pl.BlockSpec((3, None, tr, tc), lambda lb, i, j, sc_ref: (0, lb, i, j))],
            out_specs=o_spec),
        out_shape=jax.ShapeDtypeStruct((l, hr, w), F32),
        compiler_params=_cparams(("parallel", "parallel", "parallel"), 5 * _nbytes((tr, tc), F32)),
    )(sc, g_v, p_v, recv)


def _halves_exchange(halves):
    n = len(halves)

    def body(*refs):
        hv, out = refs[:n], refs[n:2 * n]
        lsem, ssem, rsem = refs[2 * n:]
        x, y, c = _coords()

        def copy(a):
            return pltpu.make_async_remote_copy(src_ref=hv[a], dst_ref=out[a].at[:, c], send_sem=ssem.at[a],
                                                recv_sem=rsem.at[a], device_id=(x, y, 1 - c), device_id_type=MESH)

        def lcopy(a):
            return pltpu.make_async_copy(hv[a], out[a].at[:, c], lsem.at[a])

        for a in range(n):
            copy(a).start()
            lcopy(a).start()
        for a in range(n):
            copy(a).wait()
            lcopy(a).wait()

    any_spec = pl.BlockSpec(memory_space=pl.ANY)
    outs = pl.pallas_call(
        body, name="grad_halves_exchange", in_specs=[any_spec] * n, out_specs=[any_spec] * n,
        out_shape=[jax.ShapeDtypeStruct((h.shape[0], 2, h.shape[1], h.shape[2]), F32) for h in halves],
        scratch_shapes=[pltpu.SemaphoreType.DMA((n,))] * 3,
    )(*halves)
    return [o.reshape(o.shape[0], 2 * o.shape[2], o.shape[3]) for o in outs]


def _small_allreduce(v, *, name):
    nr = v.shape[0]
    hr = nr // 2

    def body(v_ref, o_ref, pair, csum, got, ssem, rsem):
        x, y, c = _coords()
        peers = _chip_peers(x, y)
        sib = (x, y, 1 - c)
        mine = pl.ds(pl.multiple_of(c * hr, SUBLANES), hr)
        other = pl.ds(pl.multiple_of((1 - c) * hr, SUBLANES), hr)

        def rcopy(src, dst, k, dev):
            return pltpu.make_async_remote_copy(src_ref=src, dst_ref=dst, send_sem=ssem.at[k], recv_sem=rsem.at[k],
                                                device_id=dev, device_id_type=MESH)

        to_sib = rcopy(v_ref.at[other], pair, 0, sib)
        to_sib.start()
        to_sib.wait()
        csum[...] = v_ref[mine, :] + pair[...]
        sends = [rcopy(csum, got.at[k], 1 + k, (*peers[k], c)) for k in range(3)]
        for cp in sends:
            cp.start()
        for cp in sends:
            cp.wait()
        o_ref[mine, :] = (csum[...] + got[1]) + (got[0] + got[2])
        back = rcopy(o_ref.at[mine], o_ref.at[mine], 4, sib)
        back.start()
        back.wait()

    vm = pl.BlockSpec(memory_space=pltpu.VMEM)
    return pl.pallas_call(
        body, name=name, in_specs=[vm], out_specs=vm, out_shape=jax.ShapeDtypeStruct((nr, LANES), F32),
        scratch_shapes=[pltpu.VMEM((hr, LANES), F32), pltpu.VMEM((hr, LANES), F32), pltpu.VMEM((3, hr, LANES), F32),
                        pltpu.SemaphoreType.DMA((5,)), pltpu.SemaphoreType.DMA((5,))],
        compiler_params=pltpu.CompilerParams(vmem_limit_bytes=min(VMEM_CAP, 8 * _nbytes((nr, LANES), F32) + (8 << 20))),
    )(v)


def _pack(arrays):
    flat = jnp.concatenate([a.reshape(-1).astype(F32) for a in arrays])
    unit = 4 * SUBLANES * LANES
    n = -(-flat.shape[0] // unit) * unit
    return jnp.pad(flat, (0, n - flat.shape[0])).reshape(n // LANES, LANES)


def _unpack(packed, shapes):
    flat = packed.reshape(-1)
    out, off = [], 0
    for shp in shapes:
        sz = math.prod(shp)
        out.append(flat[off:off + sz].reshape(shp))
        off += sz
    return out


def _block_diag_groups(w, hpg):
    h, hd, _ = w.shape
    wg = w.reshape(h // hpg, hpg, hd, hd)
    eye = jnp.eye(hpg, dtype=w.dtype)
    return jnp.einsum("ghij,hk->ghikj", wg, eye).reshape(h // hpg, hpg * hd, hpg * hd)


def _diag_blocks(wd, hpg, hd):
    ngr = wd.shape[0]
    w5 = wd.reshape(ngr, hpg, hd, hpg, hd)
    return jnp.stack([w5[:, h, :, h, :] for h in range(hpg)], axis=1).reshape(ngr * hpg, hd, hd)


def kernel(x, norm_mix_g, norm_mlp_g, final_norm_g, a_w_in, a_conv_w, a_conv_b, a_gate_a_w, a_gate_a_b, a_gate_x_w, a_gate_x_b, a_lambda, a_w_out, b_w_in, b_norm_g, b_w_s, b_s_bias, b_w_out, c_w_in, c_conv_w, c_w_out, mlp_w1, mlp_w2, loss_target, m_norm_mix_g, m_norm_mlp_g, m_final_norm_g, m_a_w_in, m_a_conv_w, m_a_conv_b, m_a_gate_a_w, m_a_gate_a_b, m_a_gate_x_w, m_a_gate_x_b, m_a_lambda, m_a_w_out, m_b_w_in, m_b_norm_g, m_b_w_s, m_b_s_bias, m_b_w_out, m_c_w_in, m_c_conv_w, m_c_w_out, m_mlp_w1, m_mlp_w2, v_norm_mix_g, v_norm_mlp_g, v_final_norm_g, v_a_w_in, v_a_conv_w, v_a_conv_b, v_a_gate_a_w, v_a_gate_a_b, v_a_gate_x_w, v_a_gate_x_b, v_a_lambda, v_a_w_out, v_b_w_in, v_b_norm_g, v_b_w_s, v_b_s_bias, v_b_w_out, v_c_w_in, v_c_conv_w, v_c_w_out, v_mlp_w1, v_mlp_w2):
    weights = dict(norm_mix_g=norm_mix_g, norm_mlp_g=norm_mlp_g, final_norm_g=final_norm_g, a_w_in=a_w_in,
                   a_conv_w=a_conv_w, a_conv_b=a_conv_b, a_gate_a_w=a_gate_a_w, a_gate_a_b=a_gate_a_b,
                   a_gate_x_w=a_gate_x_w, a_gate_x_b=a_gate_x_b, a_lambda=a_lambda, a_w_out=a_w_out, b_w_in=b_w_in,
                   b_norm_g=b_norm_g, b_w_s=b_w_s, b_s_bias=b_s_bias, b_w_out=b_w_out, c_w_in=c_w_in,
                   c_conv_w=c_conv_w, c_w_out=c_w_out, mlp_w1=mlp_w1, mlp_w2=mlp_w2)
    mom_m = dict(norm_mix_g=m_norm_mix_g, norm_mlp_g=m_norm_mlp_g, final_norm_g=m_final_norm_g, a_w_in=m_a_w_in,
                 a_conv_w=m_a_conv_w, a_conv_b=m_a_conv_b, a_gate_a_w=m_a_gate_a_w, a_gate_a_b=m_a_gate_a_b,
                 a_gate_x_w=m_a_gate_x_w, a_gate_x_b=m_a_gate_x_b, a_lambda=m_a_lambda, a_w_out=m_a_w_out,
                 b_w_in=m_b_w_in, b_norm_g=m_b_norm_g, b_w_s=m_b_w_s, b_s_bias=m_b_s_bias, b_w_out=m_b_w_out,
                 c_w_in=m_c_w_in, c_conv_w=m_c_conv_w, c_w_out=m_c_w_out, mlp_w1=m_mlp_w1, mlp_w2=m_mlp_w2)
    mom_v = dict(norm_mix_g=v_norm_mix_g, norm_mlp_g=v_norm_mlp_g, final_norm_g=v_final_norm_g, a_w_in=v_a_w_in,
                 a_conv_w=v_a_conv_w, a_conv_b=v_a_conv_b, a_gate_a_w=v_a_gate_a_w, a_gate_a_b=v_a_gate_a_b,
                 a_gate_x_w=v_a_gate_x_w, a_gate_x_b=v_a_gate_x_b, a_lambda=v_a_lambda, a_w_out=v_a_w_out,
                 b_w_in=v_b_w_in, b_norm_g=v_b_norm_g, b_w_s=v_b_w_s, b_s_bias=v_b_s_bias, b_w_out=v_b_w_out,
                 c_w_in=v_c_w_in, c_conv_w=v_c_conv_w, c_w_out=v_c_w_out, mlp_w1=v_mlp_w1, mlp_w2=v_mlp_w2)
    order = list(weights)

    depth, d = norm_mix_g.shape
    n_a, n_b, n_c = a_w_in.shape[0], b_w_in.shape[0], c_w_in.shape[0]
    heads, hd = a_gate_a_w.shape[1], a_gate_a_w.shape[2]
    rnn = heads * hd
    gw = hd * LANES // math.gcd(hd, LANES)
    hpg = gw // hd
    assert rnn % gw == 0
    sgu_g, chunk = b_w_s.shape[1], b_w_s.shape[2]
    sgu = b_w_out.shape[1] * N_SHARDS
    gd = sgu // sgu_g
    assert gd % LANES == 0 and chunk % LANES == 0

    xi, yi, ci = _coords()
    sidx = 2 * xi + yi
    sc = jnp.stack([sidx, ci]).astype(jnp.int32)

    big = ["a_w_in", "a_w_out", "b_w_in", "b_w_out", "c_w_in", "c_w_out", "mlp_w1", "mlp_w2"]
    kinds = ["c", "r", "c", "r", "c", "r", "c", "r"]
    full = dict(zip(big, _gather_weights([weights[nm].astype(BF16) for nm in big], kinds)))

    small_sharded = ["a_conv_w", "a_conv_b", "a_gate_a_b", "a_gate_x_b", "a_lambda", "c_conv_w"]
    mine = _pack([weights[nm] for nm in small_sharded])
    slots = jnp.zeros((N_SHARDS,) + mine.shape, F32)
    slots = lax.dynamic_update_slice(slots, jnp.where(ci == 0, mine, 0.0)[None], (sidx, 0, 0))
    slots = _small_allreduce(slots.reshape(-1, LANES), name="gather_small").reshape((N_SHARDS,) + mine.shape)
    per_chip = [_unpack(slots[j], [weights[nm].shape for nm in small_sharded]) for j in range(N_SHARDS)]
    sfull = {nm: jnp.concatenate([per_chip[j][i] for j in range(N_SHARDS)], axis=-1) for i, nm in enumerate(small_sharded)}

    wa_d = [_block_diag_groups(a_gate_a_w[j], hpg).astype(BF16) for j in range(n_a)]
    wx_d = [_block_diag_groups(a_gate_x_w[j], hpg).astype(BF16) for j in range(n_a)]
    a_vec = [jnp.concatenate([sfull["a_conv_b"][j][None], sfull["a_gate_a_b"][j][None], sfull["a_gate_x_b"][j][None],
                              sfull["a_lambda"][j][None], jnp.zeros((SUBLANES - 4, rnn), F32)]) for j in range(n_a)]
    tril = jnp.tril(jnp.ones((chunk, chunk), bool))
    wc = [jnp.where(tril[None], b_w_s[j], 0.0) for j in range(n_b)]
    wc_b = [w.astype(BF16) for w in wc]
    wct_b = [jnp.swapaxes(w, 1, 2).astype(BF16) for w in wc]
    bias_full = [jnp.repeat(b_s_bias[j].T, gd, axis=1) for j in range(n_b)]

    xs = x[0]
    tgt = loss_target[0]
    saved = []
    for i in range(depth):
        kind, j = i % 3, i // 3
        h1 = _norm_fwd(xs, norm_mix_g[i][None], name=f"norm_mix_fwd_{i}")
        if kind == 0:
            p = _mm_nn(h1, full["a_w_in"], j, epi="plain", name=f"a_in_{i}")
            yv, hs = _a_core_fwd(p, sfull["a_conv_w"][j], a_vec[j], wa_d[j], wx_d[j], name=f"a_core_fwd_{i}")
            x1 = _mm_nn(yv, full["a_w_out"], j, epi="resid", resid=xs, name=f"a_out_{i}")
        elif kind == 1:
            p = _mm_nn(h1, full["b_w_in"], j, epi="plain", name=f"b_in_{i}")
            yv, hs = _b_core_fwd(p, b_norm_g[j][None], wc_b[j], bias_full[j], name=f"b_core_fwd_{i}"), None
            x1 = _mm_nn(yv, full["b_w_out"], j, epi="resid", resid=xs, name=f"b_out_{i}")
        else:
            p = _mm_nn(h1, full["c_w_in"], j, epi="plain", name=f"c_in_{i}")
            yv, hs = _c_core_fwd(p, sfull["c_conv_w"][j], name=f"c_core_fwd_{i}"), None
            x1 = _mm_nn(yv, full["c_w_out"], j, epi="resid", resid=xs, name=f"c_out_{i}")
        h2 = _norm_fwd(x1, norm_mlp_g[i][None], name=f"norm_mlp_fwd_{i}")
        act, sq = _mm_nn(h2, full["mlp_w1"], i, epi="sqrelu", name=f"mlp_up_{i}")
        x2 = _mm_nn(sq, full["mlp_w2"], i, epi="resid", resid=x1, name=f"mlp_down_{i}")
        saved.append(dict(x0=xs, h1=h1, p=p, y=yv, hs=hs, x1=x1, h2=h2, act=act, sq=sq))
        xs = x2

    loss_row, dx, dxb, dg_final = _loss_and_grad(xs, final_norm_g[None], tgt, name="loss_head")
    loss = lax.psum(loss_row[0, 0], ("x", "y", "c"))

    gfull = {nm: lax.empty(full[nm].shape, F32) for nm in big}
    g_small = {}
    dg_mix, dg_mlp = [None] * depth, [None] * depth
    for i in reversed(range(depth)):
        kind, j = i % 3, i // 3
        sv = saved[i]
        dact = _mm_nt(dxb, full["mlp_w2"], i, epi="relu2grad", act=sv["act"], out_dtype=BF16, name=f"mlp_down_bwd_{i}")
        gfull["mlp_w2"] = _mm_tn(sv["sq"], dxb, gfull["mlp_w2"], i, name=f"mlp_w2_grad_{i}")
        gfull["mlp_w1"] = _mm_tn(sv["h2"], dact, gfull["mlp_w1"], i, name=f"mlp_w1_grad_{i}")
        dh2 = _mm_nt(dact, full["mlp_w1"], i, epi="plain", name=f"mlp_up_bwd_{i}")
        dx, dxb, dg_mlp[i] = _norm_bwd(dh2, sv["x1"], norm_mlp_g[i][None], dx, name=f"norm_mlp_bwd_{i}")
        if kind == 0:
            dyv = _mm_nt(dxb, full["a_w_out"], j, epi="plain", name=f"a_out_bwd_{i}")
            gfull["a_w_out"] = _mm_tn(sv["y"], dxb, gfull["a_w_out"], j, name=f"a_w_out_grad_{i}")
            dp, sm, dwa, dwx = _a_core_bwd(sv["p"], sv["hs"], dyv, sfull["a_conv_w"][j], a_vec[j], wa_d[j], wx_d[j],
                                           name=f"a_core_bwd_{i}")
            g_small[("a", j)] = (sm, dwa, dwx)
            w_in = "a_w_in"
        elif kind == 1:
            dyv = _mm_nt(dxb, full["b_w_out"], j, epi="plain", name=f"b_out_bwd_{i}")
            gfull["b_w_out"] = _mm_tn(sv["y"], dxb, gfull["b_w_out"], j, name=f"b_w_out_grad_{i}")
            dp, dng, dwc, dbf = _b_core_bwd(sv["p"], dyv, b_norm_g[j][None], wc_b[j], wct_b[j], bias_full[j],
                                            name=f"b_core_bwd_{i}")
            g_small[("b", j)] = (dng, dwc, dbf)
            w_in = "b_w_in"
        else:
            dyv = _mm_nt(dxb, full["c_w_out"], j, epi="plain", name=f"c_out_bwd_{i}")
            gfull["c_w_out"] = _mm_tn(sv["y"], dxb, gfull["c_w_out"], j, name=f"c_w_out_grad_{i}")
            dp, dcw = _c_core_bwd(sv["p"], dyv, sfull["c_conv_w"][j], name=f"c_core_bwd_{i}")
            g_small[("c", j)] = (dcw,)
            w_in = "c_w_in"
        gfull[w_in] = _mm_tn(sv["h1"], dp, gfull[w_in], j, name=f"{w_in}_grad_{i}")
        dh1 = _mm_nt(dp, full[w_in], j, epi="plain", name=f"{w_in}_bwd_{i}")
        dx, dxb, dg_mix[i] = _norm_bwd(dh1, sv["x0"], norm_mix_g[i][None], dx, name=f"norm_mix_bwd_{i}")
    grad_x = dx[None]

    glist = [gfull[nm] for nm in big]
    pairs = _pair_exchange(glist, kinds)
    csums = [_pair_sum(g, pr, kd, sc, name=f"pair_sum_{nm}") for g, pr, kd, nm in zip(glist, pairs, kinds, big)]
    recvs = _chip_exchange(csums, kinds)
    halves = [_final_sum(g, pr, rc, kd, sc, name=f"final_sum_{nm}")
              for g, pr, rc, kd, nm in zip(glist, pairs, recvs, kinds, big)]
    grads = dict(zip(big, _halves_exchange(halves)))

    kca = a_conv_w.shape[1]
    kcc = c_conv_w.shape[1]
    small = {
        "norm_mix_g": jnp.concatenate(dg_mix), "norm_mlp_g": jnp.concatenate(dg_mlp), "final_norm_g": dg_final[0],
        "a_conv_w": jnp.stack([g_small[("a", j)][0][:kca] for j in range(n_a)]),
        "a_conv_b": jnp.stack([g_small[("a", j)][0][kca] for j in range(n_a)]),
        "a_gate_a_b": jnp.stack([g_small[("a", j)][0][kca + 1] for j in range(n_a)]),
        "a_gate_x_b": jnp.stack([g_small[("a", j)][0][kca + 2] for j in range(n_a)]),
        "a_lambda": jnp.stack([g_small[("a", j)][0][kca + 3] for j in range(n_a)]),
        "a_gate_a_w": jnp.stack([_diag_blocks(g_small[("a", j)][1], hpg, hd) for j in range(n_a)]),
        "a_gate_x_w": jnp.stack([_diag_blocks(g_small[("a", j)][2], hpg, hd) for j in range(n_a)]),
        "b_norm_g": jnp.concatenate([g_small[("b", j)][0] for j in range(n_b)]),
        "b_w_s": jnp.stack([jnp.where(tril[None], g_small[("b", j)][1], 0.0) for j in range(n_b)]),
        "b_s_bias": jnp.stack([g_small[("b", j)][2].reshape(chunk, sgu_g, gd).sum(-1).T for j in range(n_b)]),
        "c_conv_w": jnp.stack([g_small[("c", j)][0][:kcc] for j in range(n_c)]),
    }
    small_names = list(small)
    summed = _unpack(_small_allreduce(_pack([small[nm] for nm in small_names]), name="reduce_small"),
                     [small[nm].shape for nm in small_names])
    for nm, g in zip(small_names, summed):
        if nm in small_sharded:
            w_sh = weights[nm].shape[-1]
            g = lax.dynamic_slice_in_dim(g, sidx * w_sh, w_sh, axis=g.ndim - 1)
        grads[nm] = g

    deltas, new_m, new_v = {}, {}, {}
    for nm in order:
        deltas[nm], new_m[nm], new_v[nm] = _adam(grads[nm], weights[nm], mom_m[nm], mom_v[nm], name=f"adam_{nm}")
    return (loss, grad_x, *[grads[nm] for nm in order], *[deltas[nm] for nm in order],
            *[new_m[nm] for nm in order], *[new_v[nm] for nm in order])
```

```python
import functools
import math

import jax
import jax.numpy as jnp
from jax import lax
from jax.experimental import pallas as pl
from jax.experimental.pallas import tpu as pltpu

F32 = jnp.float32
BF16 = jnp.bfloat16
MESH = pl.DeviceIdType.MESH

LRU_C = 8.0
EPS = 1e-6
ADAM_LR = 0.001
ADAM_B1 = 0.9
ADAM_B2 = 0.999
ADAM_EPS = 1e-08
ADAM_WD = 0.01
ADAM_STEP = 10

N_SHARDS = 4
LANES = 128
SUBLANES = 8
V7X_VMEM_BYTES = 64 * 1024 * 1024
VMEM_CAP = V7X_VMEM_BYTES * 7 // 8
GELU_K = math.sqrt(2.0 / math.pi)
GELU_C = 0.044715
FLIPS = (2, 1, 3)


def _tile(dim, pref, mult=LANES):
    t = min(pref, dim) // mult * mult
    while t >= mult:
        if dim % t == 0:
            return t
        t -= mult
    return dim


def _nbytes(shape, dtype):
    return math.prod(shape) * jnp.dtype(dtype).itemsize


def _cparams(sem, *block_bytes, scratch=0):
    est = 2 * sum(block_bytes) + scratch + (6 << 20)
    assert est <= VMEM_CAP, est
    return pltpu.CompilerParams(dimension_semantics=sem, vmem_limit_bytes=VMEM_CAP)


def _sigmoid(x):
    return 1.0 / (1.0 + jnp.exp(-x))


def _gelu(x):
    return 0.5 * x * (1.0 + jnp.tanh(GELU_K * (x + GELU_C * x * x * x)))


def _gelu_and_grad(x):
    th = jnp.tanh(GELU_K * (x + GELU_C * x * x * x))
    g = 0.5 * x * (1.0 + th)
    dg = 0.5 * (1.0 + th) + 0.5 * x * (1.0 - th * th) * (GELU_K * (1.0 + 3.0 * GELU_C * x * x))
    return g, dg


def _softplus(x):
    z = jnp.exp(-jnp.abs(x))
    u = 1.0 + z
    l1p = jnp.where(u == 1.0, z, jnp.log(u) * z / (u - 1.0))
    return jnp.maximum(x, 0.0) + l1p


def _dot(a, b):
    return jnp.dot(a, b, preferred_element_type=F32)


def _dot_nt(a, b):
    return lax.dot_general(a, b, (((1,), (1,)), ((), ())), preferred_element_type=F32)


def _dot_tn(a, b):
    return lax.dot_general(a, b, (((0,), (0,)), ((), ())), preferred_element_type=F32)


def _row_sum(v):
    return jnp.sum(v, axis=0, keepdims=True)


MM_VMEM_BUDGET = 30 << 20
MM_SLAB_BYTES = 8 << 20


def _mm_tiles(m, k, n, out_bytes_per_elem):
    tn = n
    while True:
        for tm in (1024, 512, 256):
            if m % tm:
                continue
            est = 2 * (tm * k * 2 + k * tn * 2 + tm * tn * out_bytes_per_elem) + tm * tn * 4
            if k * tn * 2 <= MM_SLAB_BYTES and est <= MM_VMEM_BUDGET:
                return tm, tn
        nxt = _tile(n, tn - LANES)
        if nxt >= tn:
            return _tile(m, 256, SUBLANES), tn
        tn = nxt


def _mm_nn(a, w, l, *, epi, name, out_dtype=F32, resid=None):
    m, k = a.shape
    n = w.shape[2]
    obytes = {"plain": jnp.dtype(out_dtype).itemsize, "resid": 8, "sqrelu": 4}[epi]
    tm, tn = _mm_tiles(m, k, n, obytes)

    def body(*refs):
        a_ref, w_ref = refs[:2]
        v = _dot(a_ref[...], w_ref[...])
        if epi == "resid":
            refs[3][...] = refs[2][...] + v
        elif epi == "sqrelu":
            refs[2][...] = v.astype(BF16)
            rl = jnp.maximum(v, 0.0)
            refs[3][...] = (rl * rl).astype(BF16)
        else:
            refs[2][...] = v.astype(out_dtype)

    in_specs = [pl.BlockSpec((tm, k), lambda j, i: (i, 0)), pl.BlockSpec((None, k, tn), lambda j, i: (l, 0, j))]
    args = [a, w]
    o_spec = pl.BlockSpec((tm, tn), lambda j, i: (i, j))
    if epi == "resid":
        in_specs.append(o_spec)
        args.append(resid)
        out_shape, out_specs = jax.ShapeDtypeStruct((m, n), F32), o_spec
    elif epi == "sqrelu":
        out_shape = (jax.ShapeDtypeStruct((m, n), BF16), jax.ShapeDtypeStruct((m, n), BF16))
        out_specs = (o_spec, o_spec)
    else:
        out_shape, out_specs = jax.ShapeDtypeStruct((m, n), out_dtype), o_spec
    blocks = [tm * k * 2, k * tn * 2, tm * tn * obytes]
    return pl.pallas_call(
        body, name=name, grid=(n // tn, m // tm), in_specs=in_specs, out_specs=out_specs, out_shape=out_shape,
        compiler_params=_cparams(("parallel", "parallel"), *blocks, scratch=3 * tm * tn * 4),
    )(*args)


def _mm_nt(dy, w, l, *, epi, name, out_dtype=F32, act=None):
    m, n = dy.shape
    k = w.shape[1]
    obytes = jnp.dtype(out_dtype).itemsize + (2 if epi == "relu2grad" else 0)
    tm, tk = _mm_tiles(m, n, k, obytes)

    def body(*refs):
        d_ref, w_ref = refs[:2]
        v = _dot_nt(d_ref[...], w_ref[...])
        if epi == "relu2grad":
            v = v * (2.0 * jnp.maximum(refs[2][...].astype(F32), 0.0))
        refs[-1][...] = v.astype(out_dtype)

    in_specs = [pl.BlockSpec((tm, n), lambda j, i: (i, 0)), pl.BlockSpec((None, tk, n), lambda j, i: (l, j, 0))]
    args = [dy, w]
    o_spec = pl.BlockSpec((tm, tk), lambda j, i: (i, j))
    if epi == "relu2grad":
        in_specs.append(o_spec)
        args.append(act)
    blocks = [tm * n * 2, tk * n * 2, tm * tk * obytes]
    return pl.pallas_call(
        body, name=name, grid=(k // tk, m // tm), in_specs=in_specs, out_specs=o_spec,
        out_shape=jax.ShapeDtypeStruct((m, k), out_dtype),
        compiler_params=_cparams(("parallel", "parallel"), *blocks, scratch=3 * tm * tk * 4),
    )(*args)


def _mm_tn(a, dy, gbuf, l, *, name):
    m, k = a.shape
    n = dy.shape[1]
    tm, tk, tn = _tile(m, 1024, SUBLANES), _tile(k, 1024), _tile(n, 1024)

    def body(a_ref, d_ref, g_in, o_ref):
        del g_in
        mm = pl.program_id(2)
        v = _dot_tn(a_ref[...], d_ref[...])

        @pl.when(mm == 0)
        def _():
            o_ref[...] = v

        @pl.when(mm > 0)
        def _():
            o_ref[...] += v

    blocks = [_nbytes((tm, tk), BF16), _nbytes((tm, tn), BF16), _nbytes((tk, tn), F32)]
    return pl.pallas_call(
        body, name=name, grid=(k // tk, n // tn, m // tm),
        in_specs=[pl.BlockSpec((tm, tk), lambda i, j, mm: (mm, i)), pl.BlockSpec((tm, tn), lambda i, j, mm: (mm, j)),
                  pl.BlockSpec(memory_space=pl.ANY)],
        out_specs=pl.BlockSpec((None, tk, tn), lambda i, j, mm: (l, i, j)),
        out_shape=jax.ShapeDtypeStruct(gbuf.shape, F32), input_output_aliases={2: 0},
        compiler_params=_cparams(("parallel", "parallel", "arbitrary"), *blocks, scratch=2 * _nbytes((tk, tn), F32)),
    )(a, dy, gbuf)


def _norm_fwd(x, g, *, name):
    t, d = x.shape
    tt = _tile(t, 512, SUBLANES)

    def body(x_ref, g_ref, o_ref):
        xv = x_ref[...]
        r = lax.rsqrt(jnp.mean(xv * xv, axis=-1, keepdims=True) + EPS)
        o_ref[...] = ((xv * r) * g_ref[...]).astype(BF16)

    blk = pl.BlockSpec((tt, d), lambda i: (i, 0))
    return pl.pallas_call(
        body, name=name, grid=(t // tt,), in_specs=[blk, pl.BlockSpec((1, d), lambda i: (0, 0))], out_specs=blk,
        out_shape=jax.ShapeDtypeStruct((t, d), BF16),
        compiler_params=_cparams(("parallel",), 4 * _nbytes((tt, d), F32)),
    )(x, g)


def _norm_bwd(dh, x, g, dres, *, name):
    t, d = x.shape
    tt = _tile(t, 512, SUBLANES)

    def body(dh_ref, x_ref, g_ref, dr_ref, dx_ref, dxb_ref, dg_ref):
        @pl.when(pl.program_id(0) == 0)
        def _():
            dg_ref[...] = jnp.zeros_like(dg_ref)

        xv = x_ref[...]
        dhv = dh_ref[...]
        r = lax.rsqrt(jnp.mean(xv * xv, axis=-1, keepdims=True) + EPS)
        xh = xv * r
        dhg = dhv * g_ref[...]
        dx = dr_ref[...] + r * (dhg - xh * jnp.mean(dhg * xh, axis=-1, keepdims=True))
        dx_ref[...] = dx
        dxb_ref[...] = dx.astype(BF16)
        dg_ref[...] += _row_sum(dhv * xh)

    blk = pl.BlockSpec((tt, d), lambda i: (i, 0))
    row = pl.BlockSpec((1, d), lambda i: (0, 0))
    return pl.pallas_call(
        body, name=name, grid=(t // tt,), in_specs=[blk, blk, row, blk], out_specs=(blk, blk, row),
        out_shape=(jax.ShapeDtypeStruct((t, d), F32), jax.ShapeDtypeStruct((t, d), BF16), jax.ShapeDtypeStruct((1, d), F32)),
        compiler_params=_cparams(("arbitrary",), 8 * _nbytes((tt, d), F32)),
    )(dh, x, g, dres)


def _loss_and_grad(x, g, target, *, name):
    t, d = x.shape
    tt = _tile(t, 512, SUBLANES)
    nt = t // tt

    def body(x_ref, g_ref, t_ref, loss_ref, dx_ref, dxb_ref, dg_ref, acc):
        i = pl.program_id(0)

        @pl.when(i == 0)
        def _():
            dg_ref[...] = jnp.zeros_like(dg_ref)
            acc[...] = jnp.zeros_like(acc)

        xv = x_ref[...]
        gv = g_ref[...]
        r = lax.rsqrt(jnp.mean(xv * xv, axis=-1, keepdims=True) + EPS)
        xh = xv * r
        err = xh * gv - t_ref[...]
        acc[...] += _row_sum(err * err)
        dy = err * (1.0 / d)
        dyg = dy * gv
        dx = r * (dyg - xh * jnp.mean(dyg * xh, axis=-1, keepdims=True))
        dx_ref[...] = dx
        dxb_ref[...] = dx.astype(BF16)
        dg_ref[...] += _row_sum(dy * xh)

        @pl.when(i == nt - 1)
        def _():
            loss_ref[...] = jnp.full(loss_ref.shape, (0.5 / d) * jnp.sum(acc[...]), F32)

    blk = pl.BlockSpec((tt, d), lambda i: (i, 0))
    row = pl.BlockSpec((1, d), lambda i: (0, 0))
    return pl.pallas_call(
        body, name=name, grid=(nt,), in_specs=[blk, row, blk],
        out_specs=(pl.BlockSpec((1, LANES), lambda i: (0, 0)), blk, blk, row),
        out_shape=(jax.ShapeDtypeStruct((1, LANES), F32), jax.ShapeDtypeStruct((t, d), F32),
                   jax.ShapeDtypeStruct((t, d), BF16), jax.ShapeDtypeStruct((1, d), F32)),
        scratch_shapes=[pltpu.VMEM((1, d), F32)],
        compiler_params=_cparams(("arbitrary",), 8 * _nbytes((tt, d), F32)),
    )(x, g, target)


def _scan_fwd(a, u, tt, row):
    s = 1
    while s < tt:
        a_s = pltpu.roll(a, s, 0)
        u_s = pltpu.roll(u, s, 0)
        m = row >= s
        u = jnp.where(m, a * u_s + u, u)
        a = jnp.where(m, a * a_s, a)
        s *= 2
    return a, u


def _scan_bwd(a, u, tt, row):
    s = 1
    while s < tt:
        a_s = pltpu.roll(a, tt - s, 0)
        u_s = pltpu.roll(u, tt - s, 0)
        m = row < tt - s
        u = jnp.where(m, a * u_s + u, u)
        a = jnp.where(m, a * a_s, a)
        s *= 2
    return u


def _a_gates(xc, wa, wx, ba, bx, lam):
    xcb = xc.astype(BF16)
    ra = _sigmoid(_dot(xcb, wa) + ba)
    ia = _sigmoid(_dot(xcb, wx) + bx)
    sp = _softplus(-lam)
    la = (-LRU_C) * ra * sp
    a = jnp.exp(la)
    mult = jnp.sqrt(-jnp.tanh(la) * (a * a + 1.0))
    return xcb, ra, ia, sp, a, mult


def _a_core_fwd(p, cw, vec, wa, wx, *, name):
    t, r2 = p.shape
    r = r2 // 2
    ng, gw = wa.shape[0], wa.shape[1]
    kc = cw.shape[0]
    tt = _tile(t, 128, SUBLANES)
    nt = t // tt

    def body(p_ref, cw_ref, vec_ref, wa_ref, wx_ref, y_ref, h_ref, ext, hcar):
        i = pl.program_id(0)

        @pl.when(i == 0)
        def _():
            ext[...] = jnp.zeros_like(ext)
            hcar[...] = jnp.zeros_like(hcar)

        row = lax.broadcasted_iota(jnp.int32, (tt, gw), 0)
        for q in range(ng):
            cs = slice(q * gw, (q + 1) * gw)
            gate = p_ref[:, q * gw:(q + 1) * gw]
            xr = p_ref[:, r + q * gw:r + (q + 1) * gw]
            ext[q, SUBLANES:, :] = xr
            xc = vec_ref[0:1, cs]
            for k in range(kc):
                xc = xc + cw_ref[k:k + 1, cs] * ext[q, pl.ds(SUBLANES - (kc - 1) + k, tt), :]
            ext[q, 0:SUBLANES, :] = xr[tt - SUBLANES:, :]
            _, _, ia, _, a, mult = _a_gates(xc, wa_ref[q], wx_ref[q], vec_ref[1:2, cs], vec_ref[2:3, cs], vec_ref[3:4, cs])
            acum, hloc = _scan_fwd(a, mult * (ia * xc), tt, row)
            h = hloc + acum * hcar[0:1, cs]
            hcar[0:1, cs] = _row_sum(jnp.where(row == tt - 1, h, 0.0))
            h_ref[:, cs] = h
            y_ref[:, cs] = (h * _gelu(gate)).astype(BF16)

    blocks = [_nbytes((tt, r2), F32), _nbytes((tt, r), F32), _nbytes((tt, r), BF16), 2 * _nbytes((ng, gw, gw), BF16)]
    return pl.pallas_call(
        body, name=name, grid=(nt,),
        in_specs=[pl.BlockSpec((tt, r2), lambda i: (i, 0)), pl.BlockSpec((kc, r), lambda i: (0, 0)),
                  pl.BlockSpec((SUBLANES, r), lambda i: (0, 0)),
                  pl.BlockSpec((ng, gw, gw), lambda i: (0, 0, 0)), pl.BlockSpec((ng, gw, gw), lambda i: (0, 0, 0))],
        out_specs=(pl.BlockSpec((tt, r), lambda i: (i, 0)), pl.BlockSpec((tt, r), lambda i: (i, 0))),
        out_shape=(jax.ShapeDtypeStruct((t, r), BF16), jax.ShapeDtypeStruct((t, r), F32)),
        scratch_shapes=[pltpu.VMEM((ng, SUBLANES + tt, gw), F32), pltpu.VMEM((SUBLANES, r), F32)],
        compiler_params=_cparams(("arbitrary",), *blocks, scratch=24 * _nbytes((tt, gw), F32)),
    )(p, cw, vec, wa, wx)


def _a_core_bwd(p, hs, dy, cw, vec, wa, wx, *, name):
    t, r2 = p.shape
    r = r2 // 2
    ng, gw = wa.shape[0], wa.shape[1]
    kc = cw.shape[0]
    tt = _tile(t, 128, SUBLANES)
    nt = t // tt
    hb = tt // SUBLANES
    r_cb, r_ba, r_bx, r_lam = kc, kc + 1, kc + 2, kc + 3

    def body(p_ref, ph_ref, h_ref, hh_ref, dy_ref, cw_ref, vec_ref, wa_ref, wx_ref,
             dp_ref, sm_ref, dwa_ref, dwx_ref, ext, hext, dext, cin):
        i = pl.program_id(0)
        first = i == nt - 1
        last = i == 0

        @pl.when(last)
        def _():
            sm_ref[...] = jnp.zeros_like(sm_ref)
            dwa_ref[...] = jnp.zeros_like(dwa_ref)
            dwx_ref[...] = jnp.zeros_like(dwx_ref)
            dext[...] = jnp.zeros_like(dext)
            cin[...] = jnp.zeros_like(cin)

        row = lax.broadcasted_iota(jnp.int32, (tt, gw), 0)
        keep = jnp.where(first, 0.0, 1.0)
        for q in range(ng):
            cs = slice(q * gw, (q + 1) * gw)
            gate = p_ref[:, q * gw:(q + 1) * gw]
            xr = p_ref[:, r + q * gw:r + (q + 1) * gw]
            ext[0:SUBLANES, :] = ph_ref[:, r + q * gw:r + (q + 1) * gw] * keep
            ext[SUBLANES:, :] = xr
            xc = vec_ref[0:1, cs]
            for k in range(kc):
                xc = xc + cw_ref[k:k + 1, cs] * ext[pl.ds(SUBLANES - (kc - 1) + k, tt), :]
            lam = vec_ref[3:4, cs]
            xcb, ra, ia, sp, a, mult = _a_gates(xc, wa_ref[q], wx_ref[q], vec_ref[1:2, cs], vec_ref[2:3, cs], lam)
            hs = h_ref[:, cs]
            hext[0:SUBLANES, :] = hh_ref[:, cs] * keep
            hext[SUBLANES:, :] = hs
            hprev = hext[pl.ds(SUBLANES - 1, tt), :]
            dyv = dy_ref[:, cs]
            gl, dgl = _gelu_and_grad(gate)
            b0 = dyv * gl + jnp.where(row == tt - 1, cin[0:1, cs], 0.0)
            dh = _scan_bwd(pltpu.roll(a, tt - 1, 0), b0, tt, row)
            cin[0:1, cs] = _row_sum(jnp.where(row == 0, a * dh, 0.0))
            da = dh * hprev
            dmult = dh * (ia * xc)
            dia = dh * (mult * xc)
            dxc = dh * (mult * ia)
            dla = da * a - dmult * (a * a) / mult
            dra = dla * ((-LRU_C) * sp)
            sm_ref[r_lam:r_lam + 1, cs] += _row_sum(dla * ((-LRU_C) * ra))
            dpa = dra * ra * (1.0 - ra)
            dpx = dia * ia * (1.0 - ia)
            sm_ref[r_ba:r_ba + 1, cs] += _row_sum(dpa)
            sm_ref[r_bx:r_bx + 1, cs] += _row_sum(dpx)
            dpab = dpa.astype(BF16)
            dpxb = dpx.astype(BF16)
            dxc = dxc + _dot_nt(dpab, wa_ref[q]) + _dot_nt(dpxb, wx_ref[q])
            dwa_ref[q] += _dot_tn(xcb, dpab)
            dwx_ref[q] += _dot_tn(xcb, dpxb)
            sm_ref[r_cb:r_cb + 1, cs] += _row_sum(dxc)
            for k in range(kc):
                sm_ref[k:k + 1, cs] += _row_sum(dxc * ext[pl.ds(SUBLANES - (kc - 1) + k, tt), :])
            dext[q, 0:tt, :] = dxc
            dxr = jnp.zeros((tt, gw), F32)
            for k in range(kc):
                dxr = dxr + cw_ref[k:k + 1, cs] * dext[q, pl.ds(kc - 1 - k, tt), :]
            dext[q, tt:, :] = dxc[0:SUBLANES, :]
            dp_ref[:, q * gw:(q + 1) * gw] = (dyv * hs * dgl).astype(BF16)
            dp_ref[:, r + q * gw:r + (q + 1) * gw] = dxr.astype(BF16)

        @pl.when(first)
        def _():
            lamv = vec_ref[3:4, :]
            sm_ref[r_lam:r_lam + 1, :] = sm_ref[r_lam:r_lam + 1, :] * (-_sigmoid(-lamv))

    def tile_idx(i):
        return (nt - 1 - i, 0)

    def halo_idx(i):
        return (jnp.maximum((nt - 1 - i) * hb - 1, 0), 0)

    const2 = lambda i: (0, 0)
    const3 = lambda i: (0, 0, 0)
    blocks = [_nbytes((tt, r2), F32), 2 * _nbytes((tt, r), F32), _nbytes((tt, r2), BF16), 4 * _nbytes((ng, gw, gw), F32)]
    return pl.pallas_call(
        body, name=name, grid=(nt,),
        in_specs=[pl.BlockSpec((tt, r2), tile_idx), pl.BlockSpec((SUBLANES, r2), halo_idx),
                  pl.BlockSpec((tt, r), tile_idx), pl.BlockSpec((SUBLANES, r), halo_idx),
                  pl.BlockSpec((tt, r), tile_idx),
                  pl.BlockSpec((kc, r), const2), pl.BlockSpec((SUBLANES, r), const2),
                  pl.BlockSpec((ng, gw, gw), const3), pl.BlockSpec((ng, gw, gw), const3)],
        out_specs=(pl.BlockSpec((tt, r2), tile_idx), pl.BlockSpec((2 * SUBLANES, r), const2),
                   pl.BlockSpec((ng, gw, gw), const3), pl.BlockSpec((ng, gw, gw), const3)),
        out_shape=(jax.ShapeDtypeStruct((t, r2), BF16), jax.ShapeDtypeStruct((2 * SUBLANES, r), F32),
                   jax.ShapeDtypeStruct((ng, gw, gw), F32), jax.ShapeDtypeStruct((ng, gw, gw), F32)),
        scratch_shapes=[pltpu.VMEM((SUBLANES + tt, gw), F32), pltpu.VMEM((SUBLANES + tt, gw), F32),
                        pltpu.VMEM((ng, tt + SUBLANES, gw), F32), pltpu.VMEM((SUBLANES, r), F32)],
        compiler_params=_cparams(("arbitrary",), *blocks, scratch=40 * _nbytes((tt, gw), F32)),
    )(p, p, hs, hs, dy, cw, vec, wa, wx)


def _b_mixed(vb, wc_ref, mix_ref, tt, ch, ngr, gd):
    for n in range(tt // ch):
        for g in range(ngr):
            mix_ref[n * ch:(n + 1) * ch, g * gd:(g + 1) * gd] = _dot(wc_ref[g], vb[n * ch:(n + 1) * ch, g * gd:(g + 1) * gd])


def _b_core_fwd(p, ng_row, wc, bias_full, *, name):
    t, s2 = p.shape
    s = s2 // 2
    ngr, ch = wc.shape[0], wc.shape[1]
    gd = s // ngr
    tt = _tile(t, 2 * ch, ch)

    def body(p_ref, ng_ref, wc_ref, b_ref, y_ref, mix):
        z = _gelu(p_ref[...])
        u = z[:, :s]
        v = z[:, s:]
        rs = lax.rsqrt(jnp.mean(v * v, axis=-1, keepdims=True) + EPS)
        vb = ((v * rs) * ng_ref[...]).astype(BF16)
        _b_mixed(vb, wc_ref, mix, tt, ch, ngr, gd)
        for n in range(tt // ch):
            rows = slice(n * ch, (n + 1) * ch)
            y_ref[rows, :] = (u[rows, :] * (mix[rows, :] + b_ref[...])).astype(BF16)

    blocks = [_nbytes((tt, s2), F32), _nbytes((tt, s), BF16), _nbytes((ngr, ch, ch), BF16), _nbytes((ch, s), F32)]
    return pl.pallas_call(
        body, name=name, grid=(t // tt,),
        in_specs=[pl.BlockSpec((tt, s2), lambda i: (i, 0)), pl.BlockSpec((1, s), lambda i: (0, 0)),
                  pl.BlockSpec((ngr, ch, ch), lambda i: (0, 0, 0)), pl.BlockSpec((ch, s), lambda i: (0, 0))],
        out_specs=pl.BlockSpec((tt, s), lambda i: (i, 0)), out_shape=jax.ShapeDtypeStruct((t, s), BF16),
        scratch_shapes=[pltpu.VMEM((tt, s), F32)],
        compiler_params=_cparams(("parallel",), *blocks, scratch=12 * _nbytes((tt, s), F32)),
    )(p, ng_row, wc, bias_full)


def _b_core_bwd(p, dy, ng_row, wc, wct, bias_full, *, name):
    t, s2 = p.shape
    s = s2 // 2
    ngr, ch = wc.shape[0], wc.shape[1]
    gd = s // ngr
    tt = _tile(t, 2 * ch, ch)

    def body(p_ref, dy_ref, ng_ref, wc_ref, wct_ref, b_ref, dp_ref, dng_ref, dwc_ref, db_ref, mix, dvn):
        @pl.when(pl.program_id(0) == 0)
        def _():
            dng_ref[...] = jnp.zeros_like(dng_ref)
            dwc_ref[...] = jnp.zeros_like(dwc_ref)
            db_ref[...] = jnp.zeros_like(db_ref)

        pv = p_ref[...]
        z, dz = _gelu_and_grad(pv)
        u = z[:, :s]
        v = z[:, s:]
        rs = lax.rsqrt(jnp.mean(v * v, axis=-1, keepdims=True) + EPS)
        vh = v * rs
        ngv = ng_ref[...]
        vb = (vh * ngv).astype(BF16)
        _b_mixed(vb, wc_ref, mix, tt, ch, ngr, gd)
        dyv = dy_ref[...]
        dmx = dyv * u
        dmb = dmx.astype(BF16)
        for n in range(tt // ch):
            rows = slice(n * ch, (n + 1) * ch)
            mix[rows, :] = mix[rows, :] + b_ref[...]
            db_ref[...] += dmx[rows, :]
            for g in range(ngr):
                cols = slice(g * gd, (g + 1) * gd)
                dvn[rows, cols] = _dot(wct_ref[g], dmb[rows, cols])
                dwc_ref[g] += _dot_nt(dmb[rows, cols], vb[rows, cols])
        du = dyv * mix[...]
        dvnv = dvn[...]
        dng_ref[...] += _row_sum(dvnv * vh)
        dvh = dvnv * ngv
        dv = rs * (dvh - vh * jnp.mean(dvh * vh, axis=-1, keepdims=True))
        dp_ref[:, :s] = (du * dz[:, :s]).astype(BF16)
        dp_ref[:, s:] = (dv * dz[:, s:]).astype(BF16)

    const2 = lambda i: (0, 0)
    const3 = lambda i: (0, 0, 0)
    blocks = [_nbytes((tt, s2), F32), _nbytes((tt, s), F32), _nbytes((tt, s2), BF16),
              2 * _nbytes((ngr, ch, ch), F32), 2 * _nbytes((ch, s), F32)]
    return pl.pallas_call(
        body, name=name, grid=(t // tt,),
        in_specs=[pl.BlockSpec((tt, s2), lambda i: (i, 0)), pl.BlockSpec((tt, s), lambda i: (i, 0)),
                  pl.BlockSpec((1, s), const2), pl.BlockSpec((ngr, ch, ch), const3), pl.BlockSpec((ngr, ch, ch), const3),
                  pl.BlockSpec((ch, s), const2)],
        out_specs=(pl.BlockSpec((tt, s2), lambda i: (i, 0)), pl.BlockSpec((1, s), const2),
                   pl.BlockSpec((ngr, ch, ch), const3), pl.BlockSpec((ch, s), const2)),
        out_shape=(jax.ShapeDtypeStruct((t, s2), BF16), jax.ShapeDtypeStruct((1, s), F32),
                   jax.ShapeDtypeStruct((ngr, ch, ch), F32), jax.ShapeDtypeStruct((ch, s), F32)),
        scratch_shapes=[pltpu.VMEM((tt, s), F32), pltpu.VMEM((tt, s), F32)],
        compiler_params=_cparams(("arbitrary",), *blocks, scratch=20 * _nbytes((tt, s), F32)),
    )(p, dy, ng_row, wc, wct, bias_full)


def _c_core_fwd(p, cw, *, name):
    t, c3 = p.shape
    c = c3 // 3
    kc = cw.shape[0]
    tt = _tile(t, 256, SUBLANES)
    hb = tt // SUBLANES

    def body(p_ref, ph_ref, cw_ref, y_ref, ext):
        keep = jnp.where(pl.program_id(0) == 0, 0.0, 1.0)
        ext[0:SUBLANES, :] = ph_ref[:, c:2 * c] * ph_ref[:, 2 * c:] * keep
        ext[SUBLANES:, :] = p_ref[:, c:2 * c] * p_ref[:, 2 * c:]
        cq = jnp.zeros((tt, c), F32)
        for k in range(kc):
            cq = cq + cw_ref[k:k + 1, :] * ext[pl.ds(SUBLANES - (kc - 1) + k, tt), :]
        y_ref[...] = (p_ref[:, :c] * cq).astype(BF16)

    blocks = [_nbytes((tt, c3), F32), _nbytes((tt, c), BF16)]
    return pl.pallas_call(
        body, name=name, grid=(t // tt,),
        in_specs=[pl.BlockSpec((tt, c3), lambda i: (i, 0)),
                  pl.BlockSpec((SUBLANES, c3), lambda i: (jnp.maximum(i * hb - 1, 0), 0)),
                  pl.BlockSpec((kc, c), lambda i: (0, 0))],
        out_specs=pl.BlockSpec((tt, c), lambda i: (i, 0)), out_shape=jax.ShapeDtypeStruct((t, c), BF16),
        scratch_shapes=[pltpu.VMEM((SUBLANES + tt, c), F32)],
        compiler_params=_cparams(("parallel",), *blocks, scratch=8 * _nbytes((tt, c), F32)),
    )(p, p, cw)


def _c_core_bwd(p, dy, cw, *, name):
    t, c3 = p.shape
    c = c3 // 3
    kc = cw.shape[0]
    tt = _tile(t, 256, SUBLANES)
    hb = tt // SUBLANES
    nt = t // tt

    def body(p_ref, ph_ref, pn_ref, dy_ref, dyn_ref, cw_ref, dp_ref, dw_ref, ext, dext):
        i = pl.program_id(0)

        @pl.when(i == 0)
        def _():
            dw_ref[...] = jnp.zeros_like(dw_ref)

        keep_prev = jnp.where(i == 0, 0.0, 1.0)
        keep_next = jnp.where(i == nt - 1, 0.0, 1.0)
        gb = p_ref[:, :c]
        gc = p_ref[:, c:2 * c]
        xv = p_ref[:, 2 * c:]
        ext[0:SUBLANES, :] = ph_ref[:, c:2 * c] * ph_ref[:, 2 * c:] * keep_prev
        ext[SUBLANES:, :] = gc * xv
        dyv = dy_ref[...]
        dcq = dyv * gb
        dext[0:tt, :] = dcq
        dext[tt:, :] = dyn_ref[...] * pn_ref[:, :c] * keep_next
        cq = jnp.zeros((tt, c), F32)
        dq = jnp.zeros((tt, c), F32)
        for k in range(kc):
            tap = ext[pl.ds(SUBLANES - (kc - 1) + k, tt), :]
            cq = cq + cw_ref[k:k + 1, :] * tap
            dw_ref[k:k + 1, :] += _row_sum(dcq * tap)
            dq = dq + cw_ref[k:k + 1, :] * dext[pl.ds(kc - 1 - k, tt), :]
        dp_ref[:, :c] = (dyv * cq).astype(BF16)
        dp_ref[:, c:2 * c] = (dq * xv).astype(BF16)
        dp_ref[:, 2 * c:] = (dq * gc).astype(BF16)

    prev_idx = lambda i: (jnp.maximum(i * hb - 1, 0), 0)
    next_idx = lambda i: (jnp.minimum((i + 1) * hb, t // SUBLANES - 1), 0)
    blocks = [_nbytes((tt, c3), F32), _nbytes((tt, c), F32), _nbytes((tt, c3), BF16)]
    return pl.pallas_call(
        body, name=name, grid=(nt,),
        in_specs=[pl.BlockSpec((tt, c3), lambda i: (i, 0)), pl.BlockSpec((SUBLANES, c3), prev_idx),
                  pl.BlockSpec((SUBLANES, c3), next_idx), pl.BlockSpec((tt, c), lambda i: (i, 0)),
                  pl.BlockSpec((SUBLANES, c), next_idx), pl.BlockSpec((kc, c), lambda i: (0, 0))],
        out_specs=(pl.BlockSpec((tt, c3), lambda i: (i, 0)), pl.BlockSpec((SUBLANES, c), lambda i: (0, 0))),
        out_shape=(jax.ShapeDtypeStruct((t, c3), BF16), jax.ShapeDtypeStruct((SUBLANES, c), F32)),
        scratch_shapes=[pltpu.VMEM((SUBLANES + tt, c), F32), pltpu.VMEM((tt + SUBLANES, c), F32)],
        compiler_params=_cparams(("arbitrary",), *blocks, scratch=12 * _nbytes((tt, c), F32)),
    )(p, p, p, dy, dy, cw)


def _adam(g, w, m, v, *, name):
    shape = w.shape
    cols = shape[-1]
    rows = math.prod(shape[:-1]) if len(shape) > 1 else 1
    g2, w2, m2, v2 = (z.reshape(rows, cols) for z in (g, w, m, v))
    tr = _tile(rows, 256, SUBLANES)
    c1 = 1.0 / (1.0 - ADAM_B1 ** ADAM_STEP)
    c2 = 1.0 / (1.0 - ADAM_B2 ** ADAM_STEP)

    def body(g_ref, w_ref, m_ref, v_ref, d_ref, nm_ref, nv_ref):
        gv = g_ref[...]
        nm = ADAM_B1 * m_ref[...] + (1.0 - ADAM_B1) * gv
        nv = ADAM_B2 * v_ref[...] + (1.0 - ADAM_B2) * (gv * gv)
        d_ref[...] = -ADAM_LR * ((nm * c1) / (jnp.sqrt(nv * c2) + ADAM_EPS) + ADAM_WD * w_ref[...])
        nm_ref[...] = nm
        nv_ref[...] = nv

    blk = pl.BlockSpec((tr, cols), lambda i: (i, 0))
    sds = jax.ShapeDtypeStruct((rows, cols), F32)
    pad = _nbytes((tr, -(-cols // LANES) * LANES), F32)
    outs = pl.pallas_call(
        body, name=name, grid=(rows // tr,), in_specs=[blk] * 4, out_specs=(blk,) * 3, out_shape=(sds,) * 3,
        compiler_params=_cparams(("parallel",), 7 * pad),
    )(g2, w2, m2, v2)
    return tuple(o.reshape(shape) for o in outs)


def _coords():
    return lax.axis_index("x"), lax.axis_index("y"), lax.axis_index("c")


def _chip_peers(x, y):
    return ((1 - x, y), (x, 1 - y), (1 - x, 1 - y))


def _for_shard(s, fn):
    for j in range(N_SHARDS):
        pl.when(s == j)(functools.partial(fn, j))


def _win(ref, kind, j, h, cs):
    if kind == "c":
        return ref.at[:, h, :, pl.ds(j * cs, cs)]
    return ref.at[:, j, h]


def _cast_place(w, kind, sc, *, name):
    l, r, cs = w.shape
    tr, tc = _tile(r, 512, 16), _tile(cs, 1024)
    nb = cs // tc

    def body(sc_ref, w_ref, o_ref):
        del sc_ref
        o_ref[...] = w_ref[...].astype(BF16)

    if kind == "c":
        full_shape = (l, r, N_SHARDS * cs)
        o_spec = pl.BlockSpec((None, tr, tc), lambda lb, i, j, sc_ref: (lb, i, sc_ref[0] * nb + j))
    else:
        full_shape = (l, N_SHARDS, r, cs)
        o_spec = pl.BlockSpec((None, None, tr, tc), lambda lb, i, j, sc_ref: (lb, sc_ref[0], i, j))
    return pl.pallas_call(
        body, name=name,
        grid_spec=pltpu.PrefetchScalarGridSpec(
            num_scalar_prefetch=1, grid=(l, r // tr, nb),
            in_specs=[pl.BlockSpec((None, tr, tc), lambda lb, i, j, sc_ref: (lb, i, j))], out_specs=o_spec),
        out_shape=jax.ShapeDtypeStruct(full_shape, BF16),
        compiler_params=_cparams(("parallel", "parallel", "parallel"), 2 * _nbytes((tr, tc), F32)),
    )(sc, w)


def _gather_weights(placed, kinds, shard_shapes):
    n = len(placed)
    views = []
    for p, kind, (l, r, cs) in zip(placed, kinds, shard_shapes):
        views.append(p.reshape((l, 2, r // 2, N_SHARDS * cs) if kind == "c" else (l, N_SHARDS, 2, r // 2, cs)))

    def body(*refs):
        fu = refs[n:2 * n]
        ssem, rsem, fsem, gsem = refs[2 * n:]
        x, y, c = _coords()
        s = 2 * x + y
        peers = _chip_peers(x, y)
        sib = (x, y, 1 - c)

        def icopy(a, j, k):
            w = _win(fu[a], kinds[a], j, c, shard_shapes[a][2])
            return pltpu.make_async_remote_copy(src_ref=w, dst_ref=w, send_sem=ssem.at[a, k], recv_sem=rsem.at[a, k],
                                                device_id=(*peers[k], c), device_id_type=MESH)

        def fcopy(a, j, k, h):
            w = _win(fu[a], kinds[a], j, h, shard_shapes[a][2])
            return pltpu.make_async_remote_copy(src_ref=w, dst_ref=w, send_sem=fsem.at[a, k], recv_sem=gsem.at[a, k],
                                                device_id=sib, device_id_type=MESH)

        def run(j):
            for a in range(n):
                for k in range(3):
                    icopy(a, j, k).start()
            for k in range(3):
                for a in range(n):
                    icopy(a, j ^ FLIPS[k], k).wait_recv()
                    fcopy(a, j ^ FLIPS[k], k, c).start()
            for k in range(3):
                for a in range(n):
                    fcopy(a, j ^ FLIPS[k], k, 1 - c).wait_recv()
            for a in range(n):
                for k in range(3):
                    icopy(a, j, k).wait_send()
                    fcopy(a, j ^ FLIPS[k], k, c).wait_send()

        _for_shard(s, run)

    any_spec = pl.BlockSpec(memory_space=pl.ANY)
    outs = pl.pallas_call(
        body, name="gather_weights", in_specs=[any_spec] * n, out_specs=[any_spec] * n,
        out_shape=[jax.ShapeDtypeStruct(v.shape, BF16) for v in views],
        input_output_aliases={a: a for a in range(n)},
        scratch_shapes=[pltpu.SemaphoreType.DMA((n, 3))] * 4,
    )(*views)
    full = []
    for o, kind, (l, r, cs) in zip(outs, kinds, shard_shapes):
        full.append(o.reshape(l, r, N_SHARDS * cs) if kind == "c" else o.reshape(l, N_SHARDS * r, cs))
    return full


def _pair_exchange(grads, kinds):
    n = len(grads)
    views, half_shapes = [], []
    for g, kind in zip(grads, kinds):
        l, rr, cc = g.shape
        if kind == "c":
            views.append(g.reshape(l, 2, rr // 2, cc))
            half_shapes.append((l, rr // 2, cc))
        else:
            ks = rr // N_SHARDS
            views.append(g.reshape(l, N_SHARDS, 2, ks // 2, cc))
            half_shapes.append((l, N_SHARDS, ks // 2, cc))

    def body(*refs):
        gv, pr = refs[:n], refs[n:2 * n]
        ssem, rsem = refs[2 * n:]
        x, y, c = _coords()

        def copy(a):
            src = gv[a].at[:, 1 - c] if kinds[a] == "c" else gv[a].at[:, :, 1 - c]
            return pltpu.make_async_remote_copy(src_ref=src, dst_ref=pr[a], send_sem=ssem.at[a], recv_sem=rsem.at[a],
                                                device_id=(x, y, 1 - c), device_id_type=MESH)

        for a in range(n):
            copy(a).start()
        for a in range(n):
            copy(a).wait()

    any_spec = pl.BlockSpec(memory_space=pl.ANY)
    return pl.pallas_call(
        body, name="grad_pair_exchange", in_specs=[any_spec] * n, out_specs=[any_spec] * n,
        out_shape=[jax.ShapeDtypeStruct(hs, F32) for hs in half_shapes],
        scratch_shapes=[pltpu.SemaphoreType.DMA((n,)), pltpu.SemaphoreType.DMA((n,))],
    )(*views)


def _pair_sum(g, pair, kind, sc, *, name):
    l, rr, cc = g.shape
    if kind == "c":
        o, hr = l, rr // 2
    else:
        o, hr = l * N_SHARDS, rr // N_SHARDS // 2
    g4 = g.reshape(o, 2, hr, cc)
    p3 = pair.reshape(o, hr, cc)
    tr, tc = _tile(hr, 512, 16), _tile(cc, 1024)

    def body(sc_ref, g_ref, p_ref, o_ref):
        del sc_ref
        o_ref[...] = (g_ref[...] + p_ref[...]).astype(BF16)

    blk = pl.BlockSpec((None, tr, tc), lambda ob, i, j, sc_ref: (ob, i, j))
    out = pl.pallas_call(
        body, name=name,
        grid_spec=pltpu.PrefetchScalarGridSpec(
            num_scalar_prefetch=1, grid=(o, hr // tr, cc // tc),
            in_specs=[pl.BlockSpec((None, None, tr, tc), lambda ob, i, j, sc_ref: (ob, sc_ref[1], i, j)), blk],
            out_specs=blk),
        out_shape=jax.ShapeDtypeStruct((o, hr, cc), BF16),
        compiler_params=_cparams(("parallel", "parallel", "parallel"), 3 * _nbytes((tr, tc), F32)),
    )(sc, g4, p3)
    return out.reshape(pair.shape)


def _chip_exchange(csums, kinds):
    n = len(csums)
    views, piece_shapes = [], []
    for cs_arr, kind in zip(csums, kinds):
        if kind == "c":
            l, hr, cc = cs_arr.shape
            views.append(cs_arr)
            piece_shapes.append((l, hr, cc // N_SHARDS))
        else:
            l, _, hr, cc = cs_arr.shape
            views.append(cs_arr)
            piece_shapes.append((l, hr, cc))

    def body(*refs):
        cv, rc = refs[:n], refs[n:2 * n]
        ssem, rsem = refs[2 * n:]
        x, y, c = _coords()
        s = 2 * x + y
        peers = _chip_peers(x, y)

        def copy(a, j, k):
            if kinds[a] == "c":
                w = piece_shapes[a][2]
                src = cv[a].at[:, :, pl.ds(j * w, w)]
            else:
                src = cv[a].at[:, j]
            return pltpu.make_async_remote_copy(src_ref=src, dst_ref=rc[a].at[k], send_sem=ssem.at[a, k],
                                                recv_sem=rsem.at[a, k], device_id=(*peers[k], c), device_id_type=MESH)

        def run(j):
            for a in range(n):
                for k in range(3):
                    copy(a, j ^ FLIPS[k], k).start()
            for a in range(n):
                for k in range(3):
                    copy(a, j ^ FLIPS[k], k).wait()

        _for_shard(s, run)

    any_spec = pl.BlockSpec(memory_space=pl.ANY)
    return pl.pallas_call(
        body, name="grad_chip_exchange", in_specs=[any_spec] * n, out_specs=[any_spec] * n,
        out_shape=[jax.ShapeDtypeStruct((3, *ps), BF16) for ps in piece_shapes],
        scratch_shapes=[pltpu.SemaphoreType.DMA((n, 3)), pltpu.SemaphoreType.DMA((n, 3))],
    )(*views)


def _final_sum(g, pair, recv, kind, sc, *, name):
    l, rr, cc = g.shape
    if kind == "c":
        hr, w = rr // 2, cc // N_SHARDS
        g_v = g.reshape(l, 2, hr, cc)
        p_v = pair
        tr, tc = _tile(hr, 512, 16), _tile(w, 1024)
        nb = w // tc
        g_spec = pl.BlockSpec((None, None, tr, tc), lambda lb, i, j, sc_ref: (lb, sc_ref[1], i, sc_ref[0] * nb + j))
        p_spec = pl.BlockSpec((None, tr, tc), lambda lb, i, j, sc_ref: (lb, i, sc_ref[0] * nb + j))
    else:
        ks = rr // N_SHARDS
        hr, w = ks // 2, cc
        g_v = g.reshape(l, N_SHARDS, 2, hr, cc)
        p_v = pair
        tr, tc = _tile(hr, 512, 16), _tile(w, 1024)
        g_spec = pl.BlockSpec((None, None, None, tr, tc), lambda lb, i, j, sc_ref: (lb, sc_ref[0], sc_ref[1], i, j))
        p_spec = pl.BlockSpec((None, None, tr, tc), lambda lb, i, j, sc_ref: (lb, sc_ref[0], i, j))

    def body(sc_ref, g_ref, p_ref, r_ref, o_ref):
        del sc_ref
        own = (g_ref[...] + p_ref[...]) + r_ref[1].astype(F32)
        o_ref[...] = own + (r_ref[0].astype(F32) + r_ref[2].astype(F32))

    o_spec = pl.BlockSpec((None, None, tr, tc), lambda lb, i, j, sc_ref: (lb, sc_ref[1], i, j))
    return pl.pallas_call(
        body, name=name,
        grid_spec=pltpu.PrefetchScalarGridSpec(
            num_scalar_prefetch=1, grid=(l, hr // tr, w // tc),
            in_specs=[g_spec, p_spec, pl.BlockSpec((3, None, tr, tc), lambda lb, i, j, sc_ref: (0, lb, i, j))],
            out_specs=o_spec),
        out_shape=jax.ShapeDtypeStruct((l, 2, hr, w), F32),
        compiler_params=_cparams(("parallel", "parallel", "parallel"), 5 * _nbytes((tr, tc), F32)),
    )(sc, g_v, p_v, recv)


def _halves_exchange(halves):
    n = len(halves)

    def body(*refs):
        out = refs[n:2 * n]
        ssem, rsem = refs[2 * n:]
        x, y, c = _coords()

        def copy(a):
            return pltpu.make_async_remote_copy(src_ref=out[a].at[:, c], dst_ref=out[a].at[:, c], send_sem=ssem.at[a],
                                                recv_sem=rsem.at[a], device_id=(x, y, 1 - c), device_id_type=MESH)

        for a in range(n):
            copy(a).start()
        for a in range(n):
            copy(a).wait()

    any_spec = pl.BlockSpec(memory_space=pl.ANY)
    outs = pl.pallas_call(
        body, name="grad_halves_exchange", in_specs=[any_spec] * n, out_specs=[any_spec] * n,
        out_shape=[jax.ShapeDtypeStruct(h.shape, F32) for h in halves], input_output_aliases={a: a for a in range(n)},
        scratch_shapes=[pltpu.SemaphoreType.DMA((n,))] * 2,
    )(*halves)
    return [o.reshape(o.shape[0], 2 * o.shape[2], o.shape[3]) for o in outs]


def _small_allreduce(v, *, name):
    nr = v.shape[0]
    hr = nr // 2

    def body(v_ref, o_ref, pair, csum, got, ssem, rsem):
        x, y, c = _coords()
        peers = _chip_peers(x, y)
        sib = (x, y, 1 - c)
        mine = pl.ds(pl.multiple_of(c * hr, SUBLANES), hr)
        other = pl.ds(pl.multiple_of((1 - c) * hr, SUBLANES), hr)

        def rcopy(src, dst, k, dev):
            return pltpu.make_async_remote_copy(src_ref=src, dst_ref=dst, send_sem=ssem.at[k], recv_sem=rsem.at[k],
                                                device_id=dev, device_id_type=MESH)

        to_sib = rcopy(v_ref.at[other], pair, 0, sib)
        to_sib.start()
        to_sib.wait()
        csum[...] = v_ref[mine, :] + pair[...]
        sends = [rcopy(csum, got.at[k], 1 + k, (*peers[k], c)) for k in range(3)]
        for cp in sends:
            cp.start()
        for cp in sends:
            cp.wait()
        o_ref[mine, :] = (csum[...] + got[1]) + (got[0] + got[2])
        back = rcopy(o_ref.at[mine], o_ref.at[mine], 4, sib)
        back.start()
        back.wait()

    vm = pl.BlockSpec(memory_space=pltpu.VMEM)
    return pl.pallas_call(
        body, name=name, in_specs=[vm], out_specs=vm, out_shape=jax.ShapeDtypeStruct((nr, LANES), F32),
        scratch_shapes=[pltpu.VMEM((hr, LANES), F32), pltpu.VMEM((hr, LANES), F32), pltpu.VMEM((3, hr, LANES), F32),
                        pltpu.SemaphoreType.DMA((5,)), pltpu.SemaphoreType.DMA((5,))],
        compiler_params=pltpu.CompilerParams(vmem_limit_bytes=min(VMEM_CAP, 8 * _nbytes((nr, LANES), F32) + (8 << 20))),
    )(v)


def _pack(arrays):
    flat = jnp.concatenate([a.reshape(-1).astype(F32) for a in arrays])
    unit = 4 * SUBLANES * LANES
    n = -(-flat.shape[0] // unit) * unit
    return jnp.pad(flat, (0, n - flat.shape[0])).reshape(n // LANES, LANES)


def _unpack(packed, shapes):
    flat = packed.reshape(-1)
    out, off = [], 0
    for shp in shapes:
        sz = math.prod(shp)
        out.append(flat[off:off + sz].reshape(shp))
        off += sz
    return out


def _block_diag_groups(w, hpg):
    h, hd, _ = w.shape
    wg = w.reshape(h // hpg, hpg, hd, hd)
    eye = jnp.eye(hpg, dtype=w.dtype)
    return jnp.einsum("ghij,hk->ghikj", wg, eye).reshape(h // hpg, hpg * hd, hpg * hd)


def _diag_blocks(wd, hpg, hd):
    ngr = wd.shape[0]
    w5 = wd.reshape(ngr, hpg, hd, hpg, hd)
    return jnp.stack([w5[:, h, :, h, :] for h in range(hpg)], axis=1).reshape(ngr * hpg, hd, hd)


def kernel(x, norm_mix_g, norm_mlp_g, final_norm_g, a_w_in, a_conv_w, a_conv_b, a_gate_a_w, a_gate_a_b, a_gate_x_w, a_gate_x_b, a_lambda, a_w_out, b_w_in, b_norm_g, b_w_s, b_s_bias, b_w_out, c_w_in, c_conv_w, c_w_out, mlp_w1, mlp_w2, loss_target, m_norm_mix_g, m_norm_mlp_g, m_final_norm_g, m_a_w_in, m_a_conv_w, m_a_conv_b, m_a_gate_a_w, m_a_gate_a_b, m_a_gate_x_w, m_a_gate_x_b, m_a_lambda, m_a_w_out, m_b_w_in, m_b_norm_g, m_b_w_s, m_b_s_bias, m_b_w_out, m_c_w_in, m_c_conv_w, m_c_w_out, m_mlp_w1, m_mlp_w2, v_norm_mix_g, v_norm_mlp_g, v_final_norm_g, v_a_w_in, v_a_conv_w, v_a_conv_b, v_a_gate_a_w, v_a_gate_a_b, v_a_gate_x_w, v_a_gate_x_b, v_a_lambda, v_a_w_out, v_b_w_in, v_b_norm_g, v_b_w_s, v_b_s_bias, v_b_w_out, v_c_w_in, v_c_conv_w, v_c_w_out, v_mlp_w1, v_mlp_w2):
    weights = dict(norm_mix_g=norm_mix_g, norm_mlp_g=norm_mlp_g, final_norm_g=final_norm_g, a_w_in=a_w_in,
                   a_conv_w=a_conv_w, a_conv_b=a_conv_b, a_gate_a_w=a_gate_a_w, a_gate_a_b=a_gate_a_b,
                   a_gate_x_w=a_gate_x_w, a_gate_x_b=a_gate_x_b, a_lambda=a_lambda, a_w_out=a_w_out, b_w_in=b_w_in,
                   b_norm_g=b_norm_g, b_w_s=b_w_s, b_s_bias=b_s_bias, b_w_out=b_w_out, c_w_in=c_w_in,
                   c_conv_w=c_conv_w, c_w_out=c_w_out, mlp_w1=mlp_w1, mlp_w2=mlp_w2)
    mom_m = dict(norm_mix_g=m_norm_mix_g, norm_mlp_g=m_norm_mlp_g, final_norm_g=m_final_norm_g, a_w_in=m_a_w_in,
                 a_conv_w=m_a_conv_w, a_conv_b=m_a_conv_b, a_gate_a_w=m_a_gate_a_w, a_gate_a_b=m_a_gate_a_b,
                 a_gate_x_w=m_a_gate_x_w, a_gate_x_b=m_a_gate_x_b, a_lambda=m_a_lambda, a_w_out=m_a_w_out,
                 b_w_in=m_b_w_in, b_norm_g=m_b_norm_g, b_w_s=m_b_w_s, b_s_bias=m_b_s_bias, b_w_out=m_b_w_out,
                 c_w_in=m_c_w_in, c_conv_w=m_c_conv_w, c_w_out=m_c_w_out, mlp_w1=m_mlp_w1, mlp_w2=m_mlp_w2)
    mom_v = dict(norm_mix_g=v_norm_mix_g, norm_mlp_g=v_norm_mlp_g, final_norm_g=v_final_norm_g, a_w_in=v_a_w_in,
                 a_conv_w=v_a_conv_w, a_conv_b=v_a_conv_b, a_gate_a_w=v_a_gate_a_w, a_gate_a_b=v_a_gate_a_b,
                 a_gate_x_w=v_a_gate_x_w, a_gate_x_b=v_a_gate_x_b, a_lambda=v_a_lambda, a_w_out=v_a_w_out,
                 b_w_in=v_b_w_in, b_norm_g=v_b_norm_g, b_w_s=v_b_w_s, b_s_bias=v_b_s_bias, b_w_out=v_b_w_out,
                 c_w_in=v_c_w_in, c_conv_w=v_c_conv_w, c_w_out=v_c_w_out, mlp_w1=v_mlp_w1, mlp_w2=v_mlp_w2)
    order = list(weights)

    depth, d = norm_mix_g.shape
    n_a, n_b, n_c = a_w_in.shape[0], b_w_in.shape[0], c_w_in.shape[0]
    heads, hd = a_gate_a_w.shape[1], a_gate_a_w.shape[2]
    rnn = heads * hd
    gw = hd * LANES // math.gcd(hd, LANES)
    hpg = gw // hd
    assert rnn % gw == 0
    sgu_g, chunk = b_w_s.shape[1], b_w_s.shape[2]
    sgu = b_w_out.shape[1] * N_SHARDS
    gd = sgu // sgu_g
    assert gd % LANES == 0 and chunk % LANES == 0

    xi, yi, ci = _coords()
    sidx = 2 * xi + yi
    sc = jnp.stack([sidx, ci]).astype(jnp.int32)

    big = ["a_w_in", "a_w_out", "b_w_in", "b_w_out", "c_w_in", "c_w_out", "mlp_w1", "mlp_w2"]
    kinds = ["c", "r", "c", "r", "c", "r", "c", "r"]
    placed = [_cast_place(weights[nm], kd, sc, name=f"cast_place_{nm}") for nm, kd in zip(big, kinds)]
    full = dict(zip(big, _gather_weights(placed, kinds, [weights[nm].shape for nm in big])))

    small_sharded = ["a_conv_w", "a_conv_b", "a_gate_a_b", "a_gate_x_b", "a_lambda", "c_conv_w"]
    mine = _pack([weights[nm] for nm in small_sharded])
    slots = jnp.zeros((N_SHARDS,) + mine.shape, F32)
    slots = lax.dynamic_update_slice(slots, jnp.where(ci == 0, mine, 0.0)[None], (sidx, 0, 0))
    slots = _small_allreduce(slots.reshape(-1, LANES), name="gather_small").reshape((N_SHARDS,) + mine.shape)
    per_chip = [_unpack(slots[j], [weights[nm].shape for nm in small_sharded]) for j in range(N_SHARDS)]
    sfull = {nm: jnp.concatenate([per_chip[j][i] for j in range(N_SHARDS)], axis=-1) for i, nm in enumerate(small_sharded)}

    wa_d = [_block_diag_groups(a_gate_a_w[j], hpg).astype(BF16) for j in range(n_a)]
    wx_d = [_block_diag_groups(a_gate_x_w[j], hpg).astype(BF16) for j in range(n_a)]
    a_vec = [jnp.concatenate([sfull["a_conv_b"][j][None], sfull["a_gate_a_b"][j][None], sfull["a_gate_x_b"][j][None],
                              sfull["a_lambda"][j][None], jnp.zeros((SUBLANES - 4, rnn), F32)]) for j in range(n_a)]
    tril = jnp.tril(jnp.ones((chunk, chunk), bool))
    wc = [jnp.where(tril[None], b_w_s[j], 0.0) for j in range(n_b)]
    wc_b = [w.astype(BF16) for w in wc]
    wct_b = [jnp.swapaxes(w, 1, 2).astype(BF16) for w in wc]
    bias_full = [jnp.repeat(b_s_bias[j].T, gd, axis=1) for j in range(n_b)]

    xs = x[0]
    tgt = loss_target[0]
    saved = []
    for i in range(depth):
        kind, j = i % 3, i // 3
        h1 = _norm_fwd(xs, norm_mix_g[i][None], name=f"norm_mix_fwd_{i}")
        if kind == 0:
            p = _mm_nn(h1, full["a_w_in"], j, epi="plain", name=f"a_in_{i}")
            yv, hs = _a_core_fwd(p, sfull["a_conv_w"][j], a_vec[j], wa_d[j], wx_d[j], name=f"a_core_fwd_{i}")
            x1 = _mm_nn(yv, full["a_w_out"], j, epi="resid", resid=xs, name=f"a_out_{i}")
        elif kind == 1:
            p = _mm_nn(h1, full["b_w_in"], j, epi="plain", name=f"b_in_{i}")
            yv, hs = _b_core_fwd(p, b_norm_g[j][None], wc_b[j], bias_full[j], name=f"b_core_fwd_{i}"), None
            x1 = _mm_nn(yv, full["b_w_out"], j, epi="resid", resid=xs, name=f"b_out_{i}")
        else:
            p = _mm_nn(h1, full["c_w_in"], j, epi="plain", name=f"c_in_{i}")
            yv, hs = _c_core_fwd(p, sfull["c_conv_w"][j], name=f"c_core_fwd_{i}"), None
            x1 = _mm_nn(yv, full["c_w_out"], j, epi="resid", resid=xs, name=f"c_out_{i}")
        h2 = _norm_fwd(x1, norm_mlp_g[i][None], name=f"norm_mlp_fwd_{i}")
        act, sq = _mm_nn(h2, full["mlp_w1"], i, epi="sqrelu", name=f"mlp_up_{i}")
        x2 = _mm_nn(sq, full["mlp_w2"], i, epi="resid", resid=x1, name=f"mlp_down_{i}")
        saved.append(dict(x0=xs, h1=h1, p=p, y=yv, hs=hs, x1=x1, h2=h2, act=act, sq=sq))
        xs = x2

    loss_row, dx, dxb, dg_final = _loss_and_grad(xs, final_norm_g[None], tgt, name="loss_head")
    loss = lax.psum(loss_row[0, 0], ("x", "y", "c"))

    gfull = {nm: lax.empty(full[nm].shape, F32) for nm in big}
    g_small = {}
    dg_mix, dg_mlp = [None] * depth, [None] * depth
    for i in reversed(range(depth)):
        kind, j = i % 3, i // 3
        sv = saved[i]
        dact = _mm_nt(dxb, full["mlp_w2"], i, epi="relu2grad", act=sv["act"], out_dtype=BF16, name=f"mlp_down_bwd_{i}")
        gfull["mlp_w2"] = _mm_tn(sv["sq"], dxb, gfull["mlp_w2"], i, name=f"mlp_w2_grad_{i}")
        gfull["mlp_w1"] = _mm_tn(sv["h2"], dact, gfull["mlp_w1"], i, name=f"mlp_w1_grad_{i}")
        dh2 = _mm_nt(dact, full["mlp_w1"], i, epi="plain", name=f"mlp_up_bwd_{i}")
        dx, dxb, dg_mlp[i] = _norm_bwd(dh2, sv["x1"], norm_mlp_g[i][None], dx, name=f"norm_mlp_bwd_{i}")
        if kind == 0:
            dyv = _mm_nt(dxb, full["a_w_out"], j, epi="plain", name=f"a_out_bwd_{i}")
            gfull["a_w_out"] = _mm_tn(sv["y"], dxb, gfull["a_w_out"], j, name=f"a_w_out_grad_{i}")
            dp, sm, dwa, dwx = _a_core_bwd(sv["p"], sv["hs"], dyv, sfull["a_conv_w"][j], a_vec[j], wa_d[j], wx_d[j],
                                           name=f"a_core_bwd_{i}")
            g_small[("a", j)] = (sm, dwa, dwx)
            w_in = "a_w_in"
        elif kind == 1:
            dyv = _mm_nt(dxb, full["b_w_out"], j, epi="plain", name=f"b_out_bwd_{i}")
            gfull["b_w_out"] = _mm_tn(sv["y"], dxb, gfull["b_w_out"], j, name=f"b_w_out_grad_{i}")
            dp, dng, dwc, dbf = _b_core_bwd(sv["p"], dyv, b_norm_g[j][None], wc_b[j], wct_b[j], bias_full[j],
                                            name=f"b_core_bwd_{i}")
            g_small[("b", j)] = (dng, dwc, dbf)
            w_in = "b_w_in"
        else:
            dyv = _mm_nt(dxb, full["c_w_out"], j, epi="plain", name=f"c_out_bwd_{i}")
            gfull["c_w_out"] = _mm_tn(sv["y"], dxb, gfull["c_w_out"], j, name=f"c_w_out_grad_{i}")
            dp, dcw = _c_core_bwd(sv["p"], dyv, sfull["c_conv_w"][j], name=f"c_core_bwd_{i}")
            g_small[("c", j)] = (dcw,)
            w_in = "c_w_in"
        gfull[w_in] = _mm_tn(sv["h1"], dp, gfull[w_in], j, name=f"{w_in}_grad_{i}")
        dh1 = _mm_nt(dp, full[w_in], j, epi="plain", name=f"{w_in}_bwd_{i}")
        dx, dxb, dg_mix[i] = _norm_bwd(dh1, sv["x0"], norm_mix_g[i][None], dx, name=f"norm_mix_bwd_{i}")
    grad_x = dx[None]

    glist = [gfull[nm] for nm in big]
    pairs = _pair_exchange(glist, kinds)
    csums = [_pair_sum(g, pr, kd, sc, name=f"pair_sum_{nm}") for g, pr, kd, nm in zip(glist, pairs, kinds, big)]
    recvs = _chip_exchange(csums, kinds)
    halves = [_final_sum(g, pr, rc, kd, sc, name=f"final_sum_{nm}")
              for g, pr, rc, kd, nm in zip(glist, pairs, recvs, kinds, big)]
    grads = dict(zip(big, _halves_exchange(halves)))

    kca = a_conv_w.shape[1]
    kcc = c_conv_w.shape[1]
    small = {
        "norm_mix_g": jnp.concatenate(dg_mix), "norm_mlp_g": jnp.concatenate(dg_mlp), "final_norm_g": dg_final[0],
        "a_conv_w": jnp.stack([g_small[("a", j)][0][:kca] for j in range(n_a)]),
        "a_conv_b": jnp.stack([g_small[("a", j)][0][kca] for j in range(n_a)]),
        "a_gate_a_b": jnp.stack([g_small[("a", j)][0][kca + 1] for j in range(n_a)]),
        "a_gate_x_b": jnp.stack([g_small[("a", j)][0][kca + 2] for j in range(n_a)]),
        "a_lambda": jnp.stack([g_small[("a", j)][0][kca + 3] for j in range(n_a)]),
        "a_gate_a_w": jnp.stack([_diag_blocks(g_small[("a", j)][1], hpg, hd) for j in range(n_a)]),
        "a_gate_x_w": jnp.stack([_diag_blocks(g_small[("a", j)][2], hpg, hd) for j in range(n_a)]),
        "b_norm_g": jnp.concatenate([g_small[("b", j)][0] for j in range(n_b)]),
        "b_w_s": jnp.stack([jnp.where(tril[None], g_small[("b", j)][1], 0.0) for j in range(n_b)]),
        "b_s_bias": jnp.stack([g_small[("b", j)][2].reshape(chunk, sgu_g, gd).sum(-1).T for j in range(n_b)]),
        "c_conv_w": jnp.stack([g_small[("c", j)][0][:kcc] for j in range(n_c)]),
    }
    small_names = list(small)
    summed = _unpack(_small_allreduce(_pack([small[nm] for nm in small_names]), name="reduce_small"),
                     [small[nm].shape for nm in small_names])
    for nm, g in zip(small_names, summed):
        if nm in small_sharded:
            w_sh = weights[nm].shape[-1]
            g = lax.dynamic_slice_in_dim(g, sidx * w_sh, w_sh, axis=g.ndim - 1)
        grads[nm] = g

    deltas, new_m, new_v = {}, {}, {}
    for nm in order:
        deltas[nm], new_m[nm], new_v[nm] = _adam(grads[nm], weights[nm], mom_m[nm], mom_v[nm], name=f"adam_{nm}")
    return (loss, grad_x, *[grads[nm] for nm in order], *[deltas[nm] for nm in order],
            *[new_m[nm] for nm in order], *[new_v[nm] for nm in order])
```

```python
import functools
import math

import jax
import jax.numpy as jnp
from jax import lax
from jax.experimental import pallas as pl
from jax.experimental.pallas import tpu as pltpu

F32 = jnp.float32
BF16 = jnp.bfloat16
MESH = pl.DeviceIdType.MESH

LRU_C = 8.0
EPS = 1e-6
ADAM_LR = 0.001
ADAM_B1 = 0.9
ADAM_B2 = 0.999
ADAM_EPS = 1e-08
ADAM_WD = 0.01
ADAM_STEP = 10

N_SHARDS = 4
LANES = 128
SUBLANES = 8
V7X_VMEM_BYTES = 64 * 1024 * 1024
VMEM_CAP = V7X_VMEM_BYTES * 7 // 8
GELU_K = math.sqrt(2.0 / math.pi)
GELU_C = 0.044715
FLIPS = (2, 1, 3)


def _tile(dim, pref, mult=LANES):
    t = min(pref, dim) // mult * mult
    while t >= mult:
        if dim % t == 0:
            return t
        t -= mult
    return dim


def _nbytes(shape, dtype):
    return math.prod(shape) * jnp.dtype(dtype).itemsize


def _cparams(sem, *block_bytes, scratch=0):
    est = 2 * sum(block_bytes) + scratch + (6 << 20)
    assert est <= VMEM_CAP, est
    return pltpu.CompilerParams(dimension_semantics=sem, vmem_limit_bytes=VMEM_CAP)


def _sigmoid(x):
    return 1.0 / (1.0 + jnp.exp(-x))


def _gelu(x):
    return 0.5 * x * (1.0 + jnp.tanh(GELU_K * (x + GELU_C * x * x * x)))


def _gelu_and_grad(x):
    th = jnp.tanh(GELU_K * (x + GELU_C * x * x * x))
    g = 0.5 * x * (1.0 + th)
    dg = 0.5 * (1.0 + th) + 0.5 * x * (1.0 - th * th) * (GELU_K * (1.0 + 3.0 * GELU_C * x * x))
    return g, dg


def _softplus(x):
    z = jnp.exp(-jnp.abs(x))
    u = 1.0 + z
    l1p = jnp.where(u == 1.0, z, jnp.log(u) * z / (u - 1.0))
    return jnp.maximum(x, 0.0) + l1p


def _dot(a, b):
    return jnp.dot(a, b, preferred_element_type=F32)


def _dot_nt(a, b):
    return lax.dot_general(a, b, (((1,), (1,)), ((), ())), preferred_element_type=F32)


def _dot_tn(a, b):
    return lax.dot_general(a, b, (((0,), (0,)), ((), ())), preferred_element_type=F32)


def _row_sum(v):
    return jnp.sum(v, axis=0, keepdims=True)


MM_VMEM_BUDGET = 30 << 20
MM_SLAB_BYTES = 8 << 20


def _mm_tiles(m, k, n, out_bytes_per_elem):
    tn = n
    while True:
        for tm in (1024, 512, 256):
            if m % tm:
                continue
            est = 2 * (tm * k * 2 + k * tn * 2 + tm * tn * out_bytes_per_elem) + tm * tn * 4
            if k * tn * 2 <= MM_SLAB_BYTES and est <= MM_VMEM_BUDGET:
                return tm, tn
        nxt = _tile(n, tn - LANES)
        if nxt >= tn:
            return _tile(m, 256, SUBLANES), tn
        tn = nxt


def _mm_nn(a, w, l, *, epi, name, out_dtype=F32, resid=None):
    m, k = a.shape
    n = w.shape[2]
    obytes = {"plain": jnp.dtype(out_dtype).itemsize, "resid": 8, "sqrelu": 4}[epi]
    tm, tn = _mm_tiles(m, k, n, obytes)

    def body(*refs):
        a_ref, w_ref = refs[:2]
        v = _dot(a_ref[...], w_ref[...])
        if epi == "resid":
            refs[3][...] = refs[2][...] + v
        elif epi == "sqrelu":
            refs[2][...] = v.astype(BF16)
            rl = jnp.maximum(v, 0.0)
            refs[3][...] = (rl * rl).astype(BF16)
        else:
            refs[2][...] = v.astype(out_dtype)

    in_specs = [pl.BlockSpec((tm, k), lambda j, i: (i, 0)), pl.BlockSpec((None, k, tn), lambda j, i: (l, 0, j))]
    args = [a, w]
    o_spec = pl.BlockSpec((tm, tn), lambda j, i: (i, j))
    if epi == "resid":
        in_specs.append(o_spec)
        args.append(resid)
        out_shape, out_specs = jax.ShapeDtypeStruct((m, n), F32), o_spec
    elif epi == "sqrelu":
        out_shape = (jax.ShapeDtypeStruct((m, n), BF16), jax.ShapeDtypeStruct((m, n), BF16))
        out_specs = (o_spec, o_spec)
    else:
        out_shape, out_specs = jax.ShapeDtypeStruct((m, n), out_dtype), o_spec
    blocks = [tm * k * 2, k * tn * 2, tm * tn * obytes]
    return pl.pallas_call(
        body, name=name, grid=(n // tn, m // tm), in_specs=in_specs, out_specs=out_specs, out_shape=out_shape,
        compiler_params=_cparams(("parallel", "parallel"), *blocks, scratch=3 * tm * tn * 4),
    )(*args)


def _mm_nt(dy, w, l, *, epi, name, out_dtype=F32, act=None):
    m, n = dy.shape
    k = w.shape[1]
    obytes = jnp.dtype(out_dtype).itemsize + (2 if epi == "relu2grad" else 0)
    tm, tk = _mm_tiles(m, n, k, obytes)

    def body(*refs):
        d_ref, w_ref = refs[:2]
        v = _dot_nt(d_ref[...], w_ref[...])
        if epi == "relu2grad":
            v = v * (2.0 * jnp.maximum(refs[2][...].astype(F32), 0.0))
        refs[-1][...] = v.astype(out_dtype)

    in_specs = [pl.BlockSpec((tm, n), lambda j, i: (i, 0)), pl.BlockSpec((None, tk, n), lambda j, i: (l, j, 0))]
    args = [dy, w]
    o_spec = pl.BlockSpec((tm, tk), lambda j, i: (i, j))
    if epi == "relu2grad":
        in_specs.append(o_spec)
        args.append(act)
    blocks = [tm * n * 2, tk * n * 2, tm * tk * obytes]
    return pl.pallas_call(
        body, name=name, grid=(k // tk, m // tm), in_specs=in_specs, out_specs=o_spec,
        out_shape=jax.ShapeDtypeStruct((m, k), out_dtype),
        compiler_params=_cparams(("parallel", "parallel"), *blocks, scratch=3 * tm * tk * 4),
    )(*args)


def _mm_tn(a, dy, *, name):
    m, k = a.shape
    n = dy.shape[1]
    tm, tk, tn = _tile(m, 1024, SUBLANES), _tile(k, 1024), _tile(n, 1024)
    nm = m // tm

    def body(a_ref, d_ref, o_ref, ob_ref):
        mm = pl.program_id(2)
        v = _dot_tn(a_ref[...], d_ref[...])

        @pl.when(mm == 0)
        def _():
            o_ref[...] = v

        @pl.when(mm > 0)
        def _():
            o_ref[...] += v

        @pl.when(mm == nm - 1)
        def _():
            ob_ref[...] = o_ref[...].astype(BF16)

    blocks = [_nbytes((tm, tk), BF16), _nbytes((tm, tn), BF16), 2 * _nbytes((tk, tn), F32)]
    o_spec = pl.BlockSpec((None, tk, tn), lambda i, j, mm: (0, i, j))
    return pl.pallas_call(
        body, name=name, grid=(k // tk, n // tn, nm),
        in_specs=[pl.BlockSpec((tm, tk), lambda i, j, mm: (mm, i)), pl.BlockSpec((tm, tn), lambda i, j, mm: (mm, j))],
        out_specs=(o_spec, o_spec),
        out_shape=(jax.ShapeDtypeStruct((1, k, n), F32), jax.ShapeDtypeStruct((1, k, n), BF16)),
        compiler_params=_cparams(("parallel", "parallel", "arbitrary"), *blocks, scratch=2 * _nbytes((tk, tn), F32)),
    )(a, dy)


def _norm_fwd(x, g, *, name):
    t, d = x.shape
    tt = _tile(t, 512, SUBLANES)

    def body(x_ref, g_ref, o_ref):
        xv = x_ref[...]
        r = lax.rsqrt(jnp.mean(xv * xv, axis=-1, keepdims=True) + EPS)
        o_ref[...] = ((xv * r) * g_ref[...]).astype(BF16)

    blk = pl.BlockSpec((tt, d), lambda i: (i, 0))
    return pl.pallas_call(
        body, name=name, grid=(t // tt,), in_specs=[blk, pl.BlockSpec((1, d), lambda i: (0, 0))], out_specs=blk,
        out_shape=jax.ShapeDtypeStruct((t, d), BF16),
        compiler_params=_cparams(("parallel",), 4 * _nbytes((tt, d), F32)),
    )(x, g)


def _norm_bwd(dh, x, g, dres, *, name):
    t, d = x.shape
    tt = _tile(t, 512, SUBLANES)

    def body(dh_ref, x_ref, g_ref, dr_ref, dx_ref, dxb_ref, dg_ref):
        @pl.when(pl.program_id(0) == 0)
        def _():
            dg_ref[...] = jnp.zeros_like(dg_ref)

        xv = x_ref[...]
        dhv = dh_ref[...]
        r = lax.rsqrt(jnp.mean(xv * xv, axis=-1, keepdims=True) + EPS)
        xh = xv * r
        dhg = dhv * g_ref[...]
        dx = dr_ref[...] + r * (dhg - xh * jnp.mean(dhg * xh, axis=-1, keepdims=True))
        dx_ref[...] = dx
        dxb_ref[...] = dx.astype(BF16)
        dg_ref[...] += _row_sum(dhv * xh)

    blk = pl.BlockSpec((tt, d), lambda i: (i, 0))
    row = pl.BlockSpec((1, d), lambda i: (0, 0))
    return pl.pallas_call(
        body, name=name, grid=(t // tt,), in_specs=[blk, blk, row, blk], out_specs=(blk, blk, row),
        out_shape=(jax.ShapeDtypeStruct((t, d), F32), jax.ShapeDtypeStruct((t, d), BF16), jax.ShapeDtypeStruct((1, d), F32)),
        compiler_params=_cparams(("arbitrary",), 8 * _nbytes((tt, d), F32)),
    )(dh, x, g, dres)


def _loss_and_grad(x, g, target, *, name):
    t, d = x.shape
    tt = _tile(t, 512, SUBLANES)
    nt = t // tt

    def body(x_ref, g_ref, t_ref, loss_ref, dx_ref, dxb_ref, dg_ref, acc):
        i = pl.program_id(0)

        @pl.when(i == 0)
        def _():
            dg_ref[...] = jnp.zeros_like(dg_ref)
            acc[...] = jnp.zeros_like(acc)

        xv = x_ref[...]
        gv = g_ref[...]
        r = lax.rsqrt(jnp.mean(xv * xv, axis=-1, keepdims=True) + EPS)
        xh = xv * r
        err = xh * gv - t_ref[...]
        acc[...] += _row_sum(err * err)
        dy = err * (1.0 / d)
        dyg = dy * gv
        dx = r * (dyg - xh * jnp.mean(dyg * xh, axis=-1, keepdims=True))
        dx_ref[...] = dx
        dxb_ref[...] = dx.astype(BF16)
        dg_ref[...] += _row_sum(dy * xh)

        @pl.when(i == nt - 1)
        def _():
            loss_ref[...] = jnp.full(loss_ref.shape, (0.5 / d) * jnp.sum(acc[...]), F32)

    blk = pl.BlockSpec((tt, d), lambda i: (i, 0))
    row = pl.BlockSpec((1, d), lambda i: (0, 0))
    return pl.pallas_call(
        body, name=name, grid=(nt,), in_specs=[blk, row, blk],
        out_specs=(pl.BlockSpec((1, LANES), lambda i: (0, 0)), blk, blk, row),
        out_shape=(jax.ShapeDtypeStruct((1, LANES), F32), jax.ShapeDtypeStruct((t, d), F32),
                   jax.ShapeDtypeStruct((t, d), BF16), jax.ShapeDtypeStruct((1, d), F32)),
        scratch_shapes=[pltpu.VMEM((1, d), F32)],
        compiler_params=_cparams(("arbitrary",), 8 * _nbytes((tt, d), F32)),
    )(x, g, target)


def _scan_fwd(a, u, tt, row):
    s = 1
    while s < tt:
        a_s = pltpu.roll(a, s, 0)
        u_s = pltpu.roll(u, s, 0)
        m = row >= s
        u = jnp.where(m, a * u_s + u, u)
        a = jnp.where(m, a * a_s, a)
        s *= 2
    return a, u


def _scan_bwd(a, u, tt, row):
    s = 1
    while s < tt:
        a_s = pltpu.roll(a, tt - s, 0)
        u_s = pltpu.roll(u, tt - s, 0)
        m = row < tt - s
        u = jnp.where(m, a * u_s + u, u)
        a = jnp.where(m, a * a_s, a)
        s *= 2
    return u


def _a_gates(xc, wa, wx, ba, bx, lam):
    xcb = xc.astype(BF16)
    ra = _sigmoid(_dot(xcb, wa) + ba)
    ia = _sigmoid(_dot(xcb, wx) + bx)
    sp = _softplus(-lam)
    la = (-LRU_C) * ra * sp
    a = jnp.exp(la)
    mult = jnp.sqrt(-jnp.tanh(la) * (a * a + 1.0))
    return xcb, ra, ia, sp, a, mult


def _a_core_fwd(p, cw, vec, wa, wx, *, name):
    t, r2 = p.shape
    r = r2 // 2
    ng, gw = wa.shape[0], wa.shape[1]
    kc = cw.shape[0]
    tt = _tile(t, 128, SUBLANES)
    nt = t // tt

    def body(p_ref, cw_ref, vec_ref, wa_ref, wx_ref, y_ref, h_ref, ext, hcar):
        i = pl.program_id(0)

        @pl.when(i == 0)
        def _():
            ext[...] = jnp.zeros_like(ext)
            hcar[...] = jnp.zeros_like(hcar)

        row = lax.broadcasted_iota(jnp.int32, (tt, gw), 0)
        for q in range(ng):
            cs = slice(q * gw, (q + 1) * gw)
            gate = p_ref[:, q * gw:(q + 1) * gw]
            xr = p_ref[:, r + q * gw:r + (q + 1) * gw]
            ext[q, SUBLANES:, :] = xr
            xc = vec_ref[0:1, cs]
            for k in range(kc):
                xc = xc + cw_ref[k:k + 1, cs] * ext[q, pl.ds(SUBLANES - (kc - 1) + k, tt), :]
            ext[q, 0:SUBLANES, :] = xr[tt - SUBLANES:, :]
            _, _, ia, _, a, mult = _a_gates(xc, wa_ref[q], wx_ref[q], vec_ref[1:2, cs], vec_ref[2:3, cs], vec_ref[3:4, cs])
            acum, hloc = _scan_fwd(a, mult * (ia * xc), tt, row)
            h = hloc + acum * hcar[0:1, cs]
            hcar[0:1, cs] = _row_sum(jnp.where(row == tt - 1, h, 0.0))
            h_ref[:, cs] = h
            y_ref[:, cs] = (h * _gelu(gate)).astype(BF16)

    blocks = [_nbytes((tt, r2), F32), _nbytes((tt, r), F32), _nbytes((tt, r), BF16), 2 * _nbytes((ng, gw, gw), BF16)]
    return pl.pallas_call(
        body, name=name, grid=(nt,),
        in_specs=[pl.BlockSpec((tt, r2), lambda i: (i, 0)), pl.BlockSpec((kc, r), lambda i: (0, 0)),
                  pl.BlockSpec((SUBLANES, r), lambda i: (0, 0)),
                  pl.BlockSpec((ng, gw, gw), lambda i: (0, 0, 0)), pl.BlockSpec((ng, gw, gw), lambda i: (0, 0, 0))],
        out_specs=(pl.BlockSpec((tt, r), lambda i: (i, 0)), pl.BlockSpec((tt, r), lambda i: (i, 0))),
        out_shape=(jax.ShapeDtypeStruct((t, r), BF16), jax.ShapeDtypeStruct((t, r), F32)),
        scratch_shapes=[pltpu.VMEM((ng, SUBLANES + tt, gw), F32), pltpu.VMEM((SUBLANES, r), F32)],
        compiler_params=_cparams(("arbitrary",), *blocks, scratch=24 * _nbytes((tt, gw), F32)),
    )(p, cw, vec, wa, wx)


def _a_core_bwd(p, hs, dy, cw, vec, wa, wx, *, name):
    t, r2 = p.shape
    r = r2 // 2
    ng, gw = wa.shape[0], wa.shape[1]
    kc = cw.shape[0]
    tt = _tile(t, 128, SUBLANES)
    nt = t // tt
    hb = tt // SUBLANES
    r_cb, r_ba, r_bx, r_lam = kc, kc + 1, kc + 2, kc + 3

    def body(p_ref, ph_ref, h_ref, hh_ref, dy_ref, cw_ref, vec_ref, wa_ref, wx_ref,
             dp_ref, sm_ref, dwa_ref, dwx_ref, ext, hext, dext, cin):
        i = pl.program_id(0)
        first = i == nt - 1
        last = i == 0

        @pl.when(last)
        def _():
            sm_ref[...] = jnp.zeros_like(sm_ref)
            dwa_ref[...] = jnp.zeros_like(dwa_ref)
            dwx_ref[...] = jnp.zeros_like(dwx_ref)
            dext[...] = jnp.zeros_like(dext)
            cin[...] = jnp.zeros_like(cin)

        row = lax.broadcasted_iota(jnp.int32, (tt, gw), 0)
        keep = jnp.where(first, 0.0, 1.0)
        for q in range(ng):
            cs = slice(q * gw, (q + 1) * gw)
            gate = p_ref[:, q * gw:(q + 1) * gw]
            xr = p_ref[:, r + q * gw:r + (q + 1) * gw]
            ext[0:SUBLANES, :] = ph_ref[:, r + q * gw:r + (q + 1) * gw] * keep
            ext[SUBLANES:, :] = xr
            xc = vec_ref[0:1, cs]
            for k in range(kc):
                xc = xc + cw_ref[k:k + 1, cs] * ext[pl.ds(SUBLANES - (kc - 1) + k, tt), :]
            lam = vec_ref[3:4, cs]
            xcb, ra, ia, sp, a, mult = _a_gates(xc, wa_ref[q], wx_ref[q], vec_ref[1:2, cs], vec_ref[2:3, cs], lam)
            hs = h_ref[:, cs]
            hext[0:SUBLANES, :] = hh_ref[:, cs] * keep
            hext[SUBLANES:, :] = hs
            hprev = hext[pl.ds(SUBLANES - 1, tt), :]
            dyv = dy_ref[:, cs]
            gl, dgl = _gelu_and_grad(gate)
            b0 = dyv * gl + jnp.where(row == tt - 1, cin[0:1, cs], 0.0)
            dh = _scan_bwd(pltpu.roll(a, tt - 1, 0), b0, tt, row)
            cin[0:1, cs] = _row_sum(jnp.where(row == 0, a * dh, 0.0))
            da = dh * hprev
            dmult = dh * (ia * xc)
            dia = dh * (mult * xc)
            dxc = dh * (mult * ia)
            dla = da * a - dmult * (a * a) / mult
            dra = dla * ((-LRU_C) * sp)
            sm_ref[r_lam:r_lam + 1, cs] += _row_sum(dla * ((-LRU_C) * ra))
            dpa = dra * ra * (1.0 - ra)
            dpx = dia * ia * (1.0 - ia)
            sm_ref[r_ba:r_ba + 1, cs] += _row_sum(dpa)
            sm_ref[r_bx:r_bx + 1, cs] += _row_sum(dpx)
            dpab = dpa.astype(BF16)
            dpxb = dpx.astype(BF16)
            dxc = dxc + _dot_nt(dpab, wa_ref[q]) + _dot_nt(dpxb, wx_ref[q])
            dwa_ref[q] += _dot_tn(xcb, dpab)
            dwx_ref[q] += _dot_tn(xcb, dpxb)
            sm_ref[r_cb:r_cb + 1, cs] += _row_sum(dxc)
            for k in range(kc):
                sm_ref[k:k + 1, cs] += _row_sum(dxc * ext[pl.ds(SUBLANES - (kc - 1) + k, tt), :])
            dext[q, 0:tt, :] = dxc
            dxr = jnp.zeros((tt, gw), F32)
            for k in range(kc):
                dxr = dxr + cw_ref[k:k + 1, cs] * dext[q, pl.ds(kc - 1 - k, tt), :]
            dext[q, tt:, :] = dxc[0:SUBLANES, :]
            dp_ref[:, q * gw:(q + 1) * gw] = (dyv * hs * dgl).astype(BF16)
            dp_ref[:, r + q * gw:r + (q + 1) * gw] = dxr.astype(BF16)

        @pl.when(first)
        def _():
            lamv = vec_ref[3:4, :]
            sm_ref[r_lam:r_lam + 1, :] = sm_ref[r_lam:r_lam + 1, :] * (-_sigmoid(-lamv))

    def tile_idx(i):
        return (nt - 1 - i, 0)

    def halo_idx(i):
        return (jnp.maximum((nt - 1 - i) * hb - 1, 0), 0)

    const2 = lambda i: (0, 0)
    const3 = lambda i: (0, 0, 0)
    blocks = [_nbytes((tt, r2), F32), 2 * _nbytes((tt, r), F32), _nbytes((tt, r2), BF16), 4 * _nbytes((ng, gw, gw), F32)]
    return pl.pallas_call(
        body, name=name, grid=(nt,),
        in_specs=[pl.BlockSpec((tt, r2), tile_idx), pl.BlockSpec((SUBLANES, r2), halo_idx),
                  pl.BlockSpec((tt, r), tile_idx), pl.BlockSpec((SUBLANES, r), halo_idx),
                  pl.BlockSpec((tt, r), tile_idx),
                  pl.BlockSpec((kc, r), const2), pl.BlockSpec((SUBLANES, r), const2),
                  pl.BlockSpec((ng, gw, gw), const3), pl.BlockSpec((ng, gw, gw), const3)],
        out_specs=(pl.BlockSpec((tt, r2), tile_idx), pl.BlockSpec((2 * SUBLANES, r), const2),
                   pl.BlockSpec((ng, gw, gw), const3), pl.BlockSpec((ng, gw, gw), const3)),
        out_shape=(jax.ShapeDtypeStruct((t, r2), BF16), jax.ShapeDtypeStruct((2 * SUBLANES, r), F32),
                   jax.ShapeDtypeStruct((ng, gw, gw), F32), jax.ShapeDtypeStruct((ng, gw, gw), F32)),
        scratch_shapes=[pltpu.VMEM((SUBLANES + tt, gw), F32), pltpu.VMEM((SUBLANES + tt, gw), F32),
                        pltpu.VMEM((ng, tt + SUBLANES, gw), F32), pltpu.VMEM((SUBLANES, r), F32)],
        compiler_params=_cparams(("arbitrary",), *blocks, scratch=40 * _nbytes((tt, gw), F32)),
    )(p, p, hs, hs, dy, cw, vec, wa, wx)


def _b_mixed(vb, wc_ref, mix_ref, tt, ch, ngr, gd):
    for n in range(tt // ch):
        for g in range(ngr):
            mix_ref[n * ch:(n + 1) * ch, g * gd:(g + 1) * gd] = _dot(wc_ref[g], vb[n * ch:(n + 1) * ch, g * gd:(g + 1) * gd])


def _b_core_fwd(p, ng_row, wc, bias_full, *, name):
    t, s2 = p.shape
    s = s2 // 2
    ngr, ch = wc.shape[0], wc.shape[1]
    gd = s // ngr
    tt = _tile(t, 2 * ch, ch)

    def body(p_ref, ng_ref, wc_ref, b_ref, y_ref, mix):
        z = _gelu(p_ref[...])
        u = z[:, :s]
        v = z[:, s:]
        rs = lax.rsqrt(jnp.mean(v * v, axis=-1, keepdims=True) + EPS)
        vb = ((v * rs) * ng_ref[...]).astype(BF16)
        _b_mixed(vb, wc_ref, mix, tt, ch, ngr, gd)
        for n in range(tt // ch):
            rows = slice(n * ch, (n + 1) * ch)
            y_ref[rows, :] = (u[rows, :] * (mix[rows, :] + b_ref[...])).astype(BF16)

    blocks = [_nbytes((tt, s2), F32), _nbytes((tt, s), BF16), _nbytes((ngr, ch, ch), BF16), _nbytes((ch, s), F32)]
    return pl.pallas_call(
        body, name=name, grid=(t // tt,),
        in_specs=[pl.BlockSpec((tt, s2), lambda i: (i, 0)), pl.BlockSpec((1, s), lambda i: (0, 0)),
                  pl.BlockSpec((ngr, ch, ch), lambda i: (0, 0, 0)), pl.BlockSpec((ch, s), lambda i: (0, 0))],
        out_specs=pl.BlockSpec((tt, s), lambda i: (i, 0)), out_shape=jax.ShapeDtypeStruct((t, s), BF16),
        scratch_shapes=[pltpu.VMEM((tt, s), F32)],
        compiler_params=_cparams(("parallel",), *blocks, scratch=12 * _nbytes((tt, s), F32)),
    )(p, ng_row, wc, bias_full)


def _b_core_bwd(p, dy, ng_row, wc, wct, bias_full, *, name):
    t, s2 = p.shape
    s = s2 // 2
    ngr, ch = wc.shape[0], wc.shape[1]
    gd = s // ngr
    tt = _tile(t, 2 * ch, ch)

    def body(p_ref, dy_ref, ng_ref, wc_ref, wct_ref, b_ref, dp_ref, dng_ref, dwc_ref, db_ref, mix, dvn):
        @pl.when(pl.program_id(0) == 0)
        def _():
            dng_ref[...] = jnp.zeros_like(dng_ref)
            dwc_ref[...] = jnp.zeros_like(dwc_ref)
            db_ref[...] = jnp.zeros_like(db_ref)

        pv = p_ref[...]
        z, dz = _gelu_and_grad(pv)
        u = z[:, :s]
        v = z[:, s:]
        rs = lax.rsqrt(jnp.mean(v * v, axis=-1, keepdims=True) + EPS)
        vh = v * rs
        ngv = ng_ref[...]
        vb = (vh * ngv).astype(BF16)
        _b_mixed(vb, wc_ref, mix, tt, ch, ngr, gd)
        dyv = dy_ref[...]
        dmx = dyv * u
        dmb = dmx.astype(BF16)
        for n in range(tt // ch):
            rows = slice(n * ch, (n + 1) * ch)
            mix[rows, :] = mix[rows, :] + b_ref[...]
            db_ref[...] += dmx[rows, :]
            for g in range(ngr):
                cols = slice(g * gd, (g + 1) * gd)
                dvn[rows, cols] = _dot(wct_ref[g], dmb[rows, cols])
                dwc_ref[g] += _dot_nt(dmb[rows, cols], vb[rows, cols])
        du = dyv * mix[...]
        dvnv = dvn[...]
        dng_ref[...] += _row_sum(dvnv * vh)
        dvh = dvnv * ngv
        dv = rs * (dvh - vh * jnp.mean(dvh * vh, axis=-1, keepdims=True))
        dp_ref[:, :s] = (du * dz[:, :s]).astype(BF16)
        dp_ref[:, s:] = (dv * dz[:, s:]).astype(BF16)

    const2 = lambda i: (0, 0)
    const3 = lambda i: (0, 0, 0)
    blocks = [_nbytes((tt, s2), F32), _nbytes((tt, s), F32), _nbytes((tt, s2), BF16),
              2 * _nbytes((ngr, ch, ch), F32), 2 * _nbytes((ch, s), F32)]
    return pl.pallas_call(
        body, name=name, grid=(t // tt,),
        in_specs=[pl.BlockSpec((tt, s2), lambda i: (i, 0)), pl.BlockSpec((tt, s), lambda i: (i, 0)),
                  pl.BlockSpec((1, s), const2), pl.BlockSpec((ngr, ch, ch), const3), pl.BlockSpec((ngr, ch, ch), const3),
                  pl.BlockSpec((ch, s), const2)],
        out_specs=(pl.BlockSpec((tt, s2), lambda i: (i, 0)), pl.BlockSpec((1, s), const2),
                   pl.BlockSpec((ngr, ch, ch), const3), pl.BlockSpec((ch, s), const2)),
        out_shape=(jax.ShapeDtypeStruct((t, s2), BF16), jax.ShapeDtypeStruct((1, s), F32),
                   jax.ShapeDtypeStruct((ngr, ch, ch), F32), jax.ShapeDtypeStruct((ch, s), F32)),
        scratch_shapes=[pltpu.VMEM((tt, s), F32), pltpu.VMEM((tt, s), F32)],
        compiler_params=_cparams(("arbitrary",), *blocks, scratch=20 * _nbytes((tt, s), F32)),
    )(p, dy, ng_row, wc, wct, bias_full)


def _c_core_fwd(p, cw, *, name):
    t, c3 = p.shape
    c = c3 // 3
    kc = cw.shape[0]
    tt = _tile(t, 256, SUBLANES)
    hb = tt // SUBLANES

    def body(p_ref, ph_ref, cw_ref, y_ref, ext):
        keep = jnp.where(pl.program_id(0) == 0, 0.0, 1.0)
        ext[0:SUBLANES, :] = ph_ref[:, c:2 * c] * ph_ref[:, 2 * c:] * keep
        ext[SUBLANES:, :] = p_ref[:, c:2 * c] * p_ref[:, 2 * c:]
        cq = jnp.zeros((tt, c), F32)
        for k in range(kc):
            cq = cq + cw_ref[k:k + 1, :] * ext[pl.ds(SUBLANES - (kc - 1) + k, tt), :]
        y_ref[...] = (p_ref[:, :c] * cq).astype(BF16)

    blocks = [_nbytes((tt, c3), F32), _nbytes((tt, c), BF16)]
    return pl.pallas_call(
        body, name=name, grid=(t // tt,),
        in_specs=[pl.BlockSpec((tt, c3), lambda i: (i, 0)),
                  pl.BlockSpec((SUBLANES, c3), lambda i: (jnp.maximum(i * hb - 1, 0), 0)),
                  pl.BlockSpec((kc, c), lambda i: (0, 0))],
        out_specs=pl.BlockSpec((tt, c), lambda i: (i, 0)), out_shape=jax.ShapeDtypeStruct((t, c), BF16),
        scratch_shapes=[pltpu.VMEM((SUBLANES + tt, c), F32)],
        compiler_params=_cparams(("parallel",), *blocks, scratch=8 * _nbytes((tt, c), F32)),
    )(p, p, cw)


def _c_core_bwd(p, dy, cw, *, name):
    t, c3 = p.shape
    c = c3 // 3
    kc = cw.shape[0]
    tt = _tile(t, 256, SUBLANES)
    hb = tt // SUBLANES
    nt = t // tt

    def body(p_ref, ph_ref, pn_ref, dy_ref, dyn_ref, cw_ref, dp_ref, dw_ref, ext, dext):
        i = pl.program_id(0)

        @pl.when(i == 0)
        def _():
            dw_ref[...] = jnp.zeros_like(dw_ref)

        keep_prev = jnp.where(i == 0, 0.0, 1.0)
        keep_next = jnp.where(i == nt - 1, 0.0, 1.0)
        gb = p_ref[:, :c]
        gc = p_ref[:, c:2 * c]
        xv = p_ref[:, 2 * c:]
        ext[0:SUBLANES, :] = ph_ref[:, c:2 * c] * ph_ref[:, 2 * c:] * keep_prev
        ext[SUBLANES:, :] = gc * xv
        dyv = dy_ref[...]
        dcq = dyv * gb
        dext[0:tt, :] = dcq
        dext[tt:, :] = dyn_ref[...] * pn_ref[:, :c] * keep_next
        cq = jnp.zeros((tt, c), F32)
        dq = jnp.zeros((tt, c), F32)
        for k in range(kc):
            tap = ext[pl.ds(SUBLANES - (kc - 1) + k, tt), :]
            cq = cq + cw_ref[k:k + 1, :] * tap
            dw_ref[k:k + 1, :] += _row_sum(dcq * tap)
            dq = dq + cw_ref[k:k + 1, :] * dext[pl.ds(kc - 1 - k, tt), :]
        dp_ref[:, :c] = (dyv * cq).astype(BF16)
        dp_ref[:, c:2 * c] = (dq * xv).astype(BF16)
        dp_ref[:, 2 * c:] = (dq * gc).astype(BF16)

    prev_idx = lambda i: (jnp.maximum(i * hb - 1, 0), 0)
    next_idx = lambda i: (jnp.minimum((i + 1) * hb, t // SUBLANES - 1), 0)
    blocks = [_nbytes((tt, c3), F32), _nbytes((tt, c), F32), _nbytes((tt, c3), BF16)]
    return pl.pallas_call(
        body, name=name, grid=(nt,),
        in_specs=[pl.BlockSpec((tt, c3), lambda i: (i, 0)), pl.BlockSpec((SUBLANES, c3), prev_idx),
                  pl.BlockSpec((SUBLANES, c3), next_idx), pl.BlockSpec((tt, c), lambda i: (i, 0)),
                  pl.BlockSpec((SUBLANES, c), next_idx), pl.BlockSpec((kc, c), lambda i: (0, 0))],
        out_specs=(pl.BlockSpec((tt, c3), lambda i: (i, 0)), pl.BlockSpec((SUBLANES, c), lambda i: (0, 0))),
        out_shape=(jax.ShapeDtypeStruct((t, c3), BF16), jax.ShapeDtypeStruct((SUBLANES, c), F32)),
        scratch_shapes=[pltpu.VMEM((SUBLANES + tt, c), F32), pltpu.VMEM((tt + SUBLANES, c), F32)],
        compiler_params=_cparams(("arbitrary",), *blocks, scratch=12 * _nbytes((tt, c), F32)),
    )(p, p, p, dy, dy, cw)


def _adam(g, w, m, v, *, name):
    shape = w.shape
    cols = shape[-1]
    rows = math.prod(shape[:-1]) if len(shape) > 1 else 1
    g2, w2, m2, v2 = (z.reshape(rows, cols) for z in (g, w, m, v))
    tr = _tile(rows, 256, SUBLANES)
    c1 = 1.0 / (1.0 - ADAM_B1 ** ADAM_STEP)
    c2 = 1.0 / (1.0 - ADAM_B2 ** ADAM_STEP)

    def body(g_ref, w_ref, m_ref, v_ref, d_ref, nm_ref, nv_ref):
        gv = g_ref[...]
        nm = ADAM_B1 * m_ref[...] + (1.0 - ADAM_B1) * gv
        nv = ADAM_B2 * v_ref[...] + (1.0 - ADAM_B2) * (gv * gv)
        d_ref[...] = -ADAM_LR * ((nm * c1) / (jnp.sqrt(nv * c2) + ADAM_EPS) + ADAM_WD * w_ref[...])
        nm_ref[...] = nm
        nv_ref[...] = nv

    blk = pl.BlockSpec((tr, cols), lambda i: (i, 0))
    sds = jax.ShapeDtypeStruct((rows, cols), F32)
    pad = _nbytes((tr, -(-cols // LANES) * LANES), F32)
    outs = pl.pallas_call(
        body, name=name, grid=(rows // tr,), in_specs=[blk] * 4, out_specs=(blk,) * 3, out_shape=(sds,) * 3,
        compiler_params=_cparams(("parallel",), 7 * pad),
    )(g2, w2, m2, v2)
    return tuple(o.reshape(shape) for o in outs)


def _coords():
    return lax.axis_index("x"), lax.axis_index("y"), lax.axis_index("c")


def _chip_peers(x, y):
    return ((1 - x, y), (x, 1 - y), (1 - x, 1 - y))


def _for_shard(s, fn):
    for j in range(N_SHARDS):
        pl.when(s == j)(functools.partial(fn, j))


def _win(ref, kind, j, h, cs):
    if kind == "c":
        return ref.at[:, h, :, pl.ds(j * cs, cs)]
    return ref.at[:, j, h]


def _cast_place(w, kind, sc, *, name):
    l, r, cs = w.shape
    tr, tc = _tile(r, 512, 16), _tile(cs, 1024)
    nb = cs // tc

    def body(sc_ref, w_ref, o_ref):
        del sc_ref
        o_ref[...] = w_ref[...].astype(BF16)

    if kind == "c":
        full_shape = (l, r, N_SHARDS * cs)
        o_spec = pl.BlockSpec((None, tr, tc), lambda lb, i, j, sc_ref: (lb, i, sc_ref[0] * nb + j))
    else:
        full_shape = (l, N_SHARDS, r, cs)
        o_spec = pl.BlockSpec((None, None, tr, tc), lambda lb, i, j, sc_ref: (lb, sc_ref[0], i, j))
    return pl.pallas_call(
        body, name=name,
        grid_spec=pltpu.PrefetchScalarGridSpec(
            num_scalar_prefetch=1, grid=(l, r // tr, nb),
            in_specs=[pl.BlockSpec((None, tr, tc), lambda lb, i, j, sc_ref: (lb, i, j))], out_specs=o_spec),
        out_shape=jax.ShapeDtypeStruct(full_shape, BF16),
        compiler_params=_cparams(("parallel", "parallel", "parallel"), 2 * _nbytes((tr, tc), F32)),
    )(sc, w)


def _gather_weights(placed, kinds, shard_shapes):
    n = len(placed)
    views = []
    for p, kind, (l, r, cs) in zip(placed, kinds, shard_shapes):
        views.append(p.reshape((l, 2, r // 2, N_SHARDS * cs) if kind == "c" else (l, N_SHARDS, 2, r // 2, cs)))

    def body(*refs):
        fu = refs[n:2 * n]
        ssem, rsem, fsem, gsem = refs[2 * n:]
        x, y, c = _coords()
        s = 2 * x + y
        peers = _chip_peers(x, y)
        sib = (x, y, 1 - c)

        def icopy(a, j, k):
            w = _win(fu[a], kinds[a], j, c, shard_shapes[a][2])
            return pltpu.make_async_remote_copy(src_ref=w, dst_ref=w, send_sem=ssem.at[a, k], recv_sem=rsem.at[a, k],
                                                device_id=(*peers[k], c), device_id_type=MESH)

        def fcopy(a, j, k, h):
            w = _win(fu[a], kinds[a], j, h, shard_shapes[a][2])
            return pltpu.make_async_remote_copy(src_ref=w, dst_ref=w, send_sem=fsem.at[a, k], recv_sem=gsem.at[a, k],
                                                device_id=sib, device_id_type=MESH)

        def run(j):
            for a in range(n):
                for k in range(3):
                    icopy(a, j, k).start()
            for k in range(3):
                for a in range(n):
                    icopy(a, j ^ FLIPS[k], k).wait_recv()
                    fcopy(a, j ^ FLIPS[k], k, c).start()
            for k in range(3):
                for a in range(n):
                    fcopy(a, j ^ FLIPS[k], k, 1 - c).wait_recv()
            for a in range(n):
                for k in range(3):
                    icopy(a, j, k).wait_send()
                    fcopy(a, j ^ FLIPS[k], k, c).wait_send()

        _for_shard(s, run)

    any_spec = pl.BlockSpec(memory_space=pl.ANY)
    outs = pl.pallas_call(
        body, name="gather_weights", in_specs=[any_spec] * n, out_specs=[any_spec] * n,
        out_shape=[jax.ShapeDtypeStruct(v.shape, BF16) for v in views],
        input_output_aliases={a: a for a in range(n)},
        scratch_shapes=[pltpu.SemaphoreType.DMA((n, 3))] * 4,
    )(*views)
    full = []
    for o, kind, (l, r, cs) in zip(outs, kinds, shard_shapes):
        full.append(o.reshape(l, r, N_SHARDS * cs) if kind == "c" else o.reshape(l, N_SHARDS * r, cs))
    return full


def _pair_exchange(grads, kinds):
    n = len(grads)
    views, half_shapes = [], []
    for g, kind in zip(grads, kinds):
        l, rr, cc = g.shape
        if kind == "c":
            views.append(g.reshape(l, 2, rr // 2, cc))
            half_shapes.append((l, rr // 2, cc))
        else:
            ks = rr // N_SHARDS
            views.append(g.reshape(l, N_SHARDS, 2, ks // 2, cc))
            half_shapes.append((l, N_SHARDS, ks // 2, cc))

    def body(*refs):
        gv, pr = refs[:n], refs[n:2 * n]
        ssem, rsem = refs[2 * n:]
        x, y, c = _coords()

        def copy(a):
            src = gv[a].at[:, 1 - c] if kinds[a] == "c" else gv[a].at[:, :, 1 - c]
            return pltpu.make_async_remote_copy(src_ref=src, dst_ref=pr[a], send_sem=ssem.at[a], recv_sem=rsem.at[a],
                                                device_id=(x, y, 1 - c), device_id_type=MESH)

        for a in range(n):
            copy(a).start()
        for a in range(n):
            copy(a).wait()

    any_spec = pl.BlockSpec(memory_space=pl.ANY)
    return pl.pallas_call(
        body, name="grad_pair_exchange", in_specs=[any_spec] * n, out_specs=[any_spec] * n,
        out_shape=[jax.ShapeDtypeStruct(hs, F32) for hs in half_shapes],
        scratch_shapes=[pltpu.SemaphoreType.DMA((n,)), pltpu.SemaphoreType.DMA((n,))],
    )(*views)


def _pair_sum(g, pair, kind, sc, *, name):
    l, rr, cc = g.shape
    if kind == "c":
        o, hr = l, rr // 2
    else:
        o, hr = l * N_SHARDS, rr // N_SHARDS // 2
    g4 = g.reshape(o, 2, hr, cc)
    p3 = pair.reshape(o, hr, cc)
    tr, tc = _tile(hr, 512, 16), _tile(cc, 1024)

    def body(sc_ref, g_ref, p_ref, o_ref):
        del sc_ref
        o_ref[...] = (g_ref[...] + p_ref[...]).astype(BF16)

    blk = pl.BlockSpec((None, tr, tc), lambda ob, i, j, sc_ref: (ob, i, j))
    out = pl.pallas_call(
        body, name=name,
        grid_spec=pltpu.PrefetchScalarGridSpec(
            num_scalar_prefetch=1, grid=(o, hr // tr, cc // tc),
            in_specs=[pl.BlockSpec((None, None, tr, tc), lambda ob, i, j, sc_ref: (ob, sc_ref[1], i, j)), blk],
            out_specs=blk),
        out_shape=jax.ShapeDtypeStruct((o, hr, cc), BF16),
        compiler_params=_cparams(("parallel", "parallel", "parallel"), 3 * _nbytes((tr, tc), F32)),
    )(sc, g4, p3)
    return out.reshape(pair.shape)


def _chip_exchange(csums, kinds):
    n = len(csums)
    views, piece_shapes = [], []
    for cs_arr, kind in zip(csums, kinds):
        if kind == "c":
            l, hr, cc = cs_arr.shape
            views.append(cs_arr)
            piece_shapes.append((l, hr, cc // N_SHARDS))
        else:
            l, _, hr, cc = cs_arr.shape
            views.append(cs_arr)
            piece_shapes.append((l, hr, cc))

    def body(*refs):
        cv, rc = refs[:n], refs[n:2 * n]
        ssem, rsem = refs[2 * n:]
        x, y, c = _coords()
        s = 2 * x + y
        peers = _chip_peers(x, y)

        def copy(a, j, k):
            if kinds[a] == "c":
                w = piece_shapes[a][2]
                src = cv[a].at[:, :, pl.ds(j * w, w)]
            else:
                src = cv[a].at[:, j]
            return pltpu.make_async_remote_copy(src_ref=src, dst_ref=rc[a].at[k], send_sem=ssem.at[a, k],
                                                recv_sem=rsem.at[a, k], device_id=(*peers[k], c), device_id_type=MESH)

        def run(j):
            for a in range(n):
                for k in range(3):
                    copy(a, j ^ FLIPS[k], k).start()
            for a in range(n):
                for k in range(3):
                    copy(a, j ^ FLIPS[k], k).wait()

        _for_shard(s, run)

    any_spec = pl.BlockSpec(memory_space=pl.ANY)
    return pl.pallas_call(
        body, name="grad_chip_exchange", in_specs=[any_spec] * n, out_specs=[any_spec] * n,
        out_shape=[jax.ShapeDtypeStruct((3, *ps), BF16) for ps in piece_shapes],
        scratch_shapes=[pltpu.SemaphoreType.DMA((n, 3)), pltpu.SemaphoreType.DMA((n, 3))],
    )(*views)


def _final_sum(g, pair, recv, kind, sc, *, name):
    l, rr, cc = g.shape
    if kind == "c":
        hr, w = rr // 2, cc // N_SHARDS
        g_v = g.reshape(l, 2, hr, cc)
        p_v = pair
        tr, tc = _tile(hr, 512, 16), _tile(w, 1024)
        nb = w // tc
        g_spec = pl.BlockSpec((None, None, tr, tc), lambda lb, i, j, sc_ref: (lb, sc_ref[1], i, sc_ref[0] * nb + j))
        p_spec = pl.BlockSpec((None, tr, tc), lambda lb, i, j, sc_ref: (lb, i, sc_ref[0] * nb + j))
    else:
        ks = rr // N_SHARDS
        hr, w = ks // 2, cc
        g_v = g.reshape(l, N_SHARDS, 2, hr, cc)
        p_v = pair
        tr, tc = _tile(hr, 512, 16), _tile(w, 1024)
        g_spec = pl.BlockSpec((None, None, None, tr, tc), lambda lb, i, j, sc_ref: (lb, sc_ref[0], sc_ref[1], i, j))
        p_spec = pl.BlockSpec((None, None, tr, tc), lambda lb, i, j, sc_ref: (lb, sc_ref[0], i, j))

    def body(sc_ref, g_ref, p_ref, r_ref, o_ref):
        del sc_ref
        own = (g_ref[...] + p_ref[...]) + r_ref[1].astype(F32)
        o_ref[...] = own + (r_ref[0].astype(F32) + r_ref[2].astype(F32))

    o_spec = pl.BlockSpec((None, None, tr, tc), lambda lb, i, j, sc_ref: (lb, sc_ref[1], i, j))
    return pl.pallas_call(
        body, name=name,
        grid_spec=pltpu.PrefetchScalarGridSpec(
            num_scalar_prefetch=1, grid=(l, hr // tr, w // tc),
            in_specs=[g_spec, p_spec, pl.BlockSpec((3, None, tr, tc), lambda lb, i, j, sc_ref: (0, lb, i, j))],
            out_specs=o_spec),
        out_shape=jax.ShapeDtypeStruct((l, 2, hr, w), F32),
        compiler_params=_cparams(("parallel", "parallel", "parallel"), 5 * _nbytes((tr, tc), F32)),
    )(sc, g_v, p_v, recv)


def _halves_exchange(halves):
    n = len(halves)

    def body(*refs):
        out = refs[n:2 * n]
        ssem, rsem = refs[2 * n:]
        x, y, c = _coords()

        def copy(a):
            return pltpu.make_async_remote_copy(src_ref=out[a].at[:, c], dst_ref=out[a].at[:, c], send_sem=ssem.at[a],
                                                recv_sem=rsem.at[a], device_id=(x, y, 1 - c), device_id_type=MESH)

        for a in range(n):
            copy(a).start()
        for a in range(n):
            copy(a).wait()

    any_spec = pl.BlockSpec(memory_space=pl.ANY)
    outs = pl.pallas_call(
        body, name="grad_halves_exchange", in_specs=[any_spec] * n, out_specs=[any_spec] * n,
        out_shape=[jax.ShapeDtypeStruct(h.shape, F32) for h in halves], input_output_aliases={a: a for a in range(n)},
        scratch_shapes=[pltpu.SemaphoreType.DMA((n,))] * 2,
    )(*halves)
    return [o.reshape(o.shape[0], 2 * o.shape[2], o.shape[3]) for o in outs]


HBM_SPEC = pl.BlockSpec(memory_space=pltpu.HBM)
SEM_SPEC = pl.BlockSpec(memory_space=pltpu.SEMAPHORE)
ANY_SPEC = pl.BlockSpec(memory_space=pl.ANY)
SIDE_EFFECT = pltpu.SideEffectType.DATAFLOW_SIDE_EFFECTING
GATHER_SLOTS = 6
REDUCE_SLOTS = 7


def _lwin(ref, kind, shard_rc, j, h):
    r, cs = shard_rc
    hr = r // 2
    if kind == "c":
        row0 = h * hr
        return ref.at[0, pl.ds(row0 if isinstance(row0, int) else pl.multiple_of(row0, 16), hr), pl.ds(j * cs, cs)]
    row0 = j * r + h * hr
    return ref.at[0, pl.ds(row0 if isinstance(row0, int) else pl.multiple_of(row0, 16), hr), :]


def _cast_place_layer(w, l, kind, sc, *, name):
    _, r, cs = w.shape
    tr, tc = _tile(r, 512, 16), _tile(cs, 1024)
    nbr, nbc = r // tr, cs // tc

    def body(sc_ref, w_ref, o_ref):
        del sc_ref
        o_ref[...] = w_ref[...].astype(BF16)

    if kind == "c":
        full_shape = (1, r, N_SHARDS * cs)
        o_spec = pl.BlockSpec((None, tr, tc), lambda i, j, sc_ref: (0, i, sc_ref[0] * nbc + j))
    else:
        full_shape = (1, N_SHARDS * r, cs)
        o_spec = pl.BlockSpec((None, tr, tc), lambda i, j, sc_ref: (0, sc_ref[0] * nbr + i, j))
    return pl.pallas_call(
        body, name=name,
        grid_spec=pltpu.PrefetchScalarGridSpec(
            num_scalar_prefetch=1, grid=(nbr, nbc),
            in_specs=[pl.BlockSpec((None, tr, tc), lambda i, j, sc_ref: (l, i, j))], out_specs=o_spec),
        out_shape=jax.ShapeDtypeStruct(full_shape, BF16),
        compiler_params=_cparams(("parallel", "parallel"), 2 * _nbytes((tr, tc), F32)),
    )(sc, w)


def _gather_plan(kinds, shards, refs, j, c, peers, sib):
    del sib
    sends, recvs = [], []
    for a, ref in enumerate(refs):
        mine = _lwin(ref, kinds[a], shards[a], j, c)
        for k in range(3):
            for h in range(2):
                base = a * GATHER_SLOTS + 2 * k
                sends.append((mine, mine, (*peers[k], h), base + h, base + c))
                recvs.append((_lwin(ref, kinds[a], shards[a], j ^ FLIPS[k], h), base + h))
    return sends, recvs


def _reduce_plan(kinds, shards, refs, j, c, peers, sib):
    n = len(refs) // 2
    sends, recvs = [], []
    for a in range(n):
        part, land = refs[a], refs[n + a]
        for k in range(3):
            for h in range(2):
                src = _lwin(part, kinds[a], shards[a], j ^ FLIPS[k], h)
                base = a * REDUCE_SLOTS + 2 * k
                sends.append((src, land.at[2 * k + c], (*peers[k], h), base + h, base + c))
        sends.append((_lwin(part, kinds[a], shards[a], j, 1 - c), land.at[6], sib, a * REDUCE_SLOTS + 6,
                      a * REDUCE_SLOTS + 6))
        recvs += [(land.at[t], a * REDUCE_SLOTS + t) for t in range(REDUCE_SLOTS)]
    return sends, recvs


def _xfer_start(arrays, plan, n_sems, after, *, name):
    n = len(arrays)
    n_after = 0 if after is None else 1

    def body(*refs):
        ins = refs[:n]
        ssem, rsem = refs[n + n_after], refs[n + n_after + 1]
        token = refs[-1]
        x, y, c = _coords()
        peers = _chip_peers(x, y)

        def run(j):
            for src, dst, dev, si, ri in plan(ins, j, c, peers, (x, y, 1 - c))[0]:
                pltpu.make_async_remote_copy(src_ref=src, dst_ref=dst, send_sem=ssem.at[si], recv_sem=rsem.at[ri],
                                             device_id=dev, device_id_type=MESH).start()

        _for_shard(2 * x + y, run)
        token[...] = jnp.zeros_like(token)

    sem_t = pltpu.SemaphoreType.DMA((n_sems,))
    outs = pl.pallas_call(
        body, name=name,
        out_shape=(sem_t, sem_t, *[pltpu.HBM(v.shape, v.dtype) for v in arrays],
                   jax.ShapeDtypeStruct((SUBLANES, LANES), F32)),
        in_specs=[HBM_SPEC] * n + [ANY_SPEC] * n_after,
        out_specs=(SEM_SPEC, SEM_SPEC, *[HBM_SPEC] * n, pl.BlockSpec(memory_space=pltpu.VMEM)),
        input_output_aliases={a: 2 + a for a in range(n)},
        compiler_params=pltpu.CompilerParams(has_side_effects=SIDE_EFFECT),
    )(*[pltpu.with_memory_space_constraint(v, pltpu.HBM) for v in arrays], *([] if after is None else [after]))
    return outs[0], outs[1], list(outs[2:2 + n]), outs[-1]


def _xfer_wait(ssem, rsem, arrays, plan, after, *, name):
    n = len(arrays)

    def body(*refs):
        ins = refs[:n]
        ssem_ref, rsem_ref = refs[n], refs[n + 1]
        x, y, c = _coords()
        peers = _chip_peers(x, y)

        def run(j):
            sends, recvs = plan(ins, j, c, peers, (x, y, 1 - c))
            for src, dst, dev, si, ri in sends:
                pltpu.make_async_remote_copy(src_ref=src, dst_ref=dst, send_sem=ssem_ref.at[si], recv_sem=rsem_ref.at[ri],
                                             device_id=dev, device_id_type=MESH).wait_send()
            for dst, ri in recvs:
                pltpu.make_async_remote_copy(src_ref=dst, dst_ref=dst, send_sem=ssem_ref.at[ri], recv_sem=rsem_ref.at[ri],
                                             device_id=(x, y, c), device_id_type=MESH).wait_recv()

        _for_shard(2 * x + y, run)

    outs = pl.pallas_call(
        body, name=name, out_shape=[pltpu.HBM(v.shape, v.dtype) for v in arrays],
        in_specs=[HBM_SPEC] * n + [SEM_SPEC, SEM_SPEC, ANY_SPEC], out_specs=[HBM_SPEC] * n,
        input_output_aliases={a: a for a in range(n)},
        compiler_params=pltpu.CompilerParams(has_side_effects=SIDE_EFFECT),
    )(*arrays, ssem, rsem, after)
    return list(outs)


def _final_sum_layer(g, land, gsh, l, kind, shard_rc, sc, *, name):
    r, cs = shard_rc
    hr = r // 2
    tr, tc = _tile(hr, 512, 16), _tile(cs, 1024)
    nbr, nbc = hr // tr, cs // tc
    if kind == "c":
        g_spec = pl.BlockSpec((None, tr, tc), lambda i, j, sc_ref: (0, sc_ref[1] * nbr + i, sc_ref[0] * nbc + j))
    else:
        g_spec = pl.BlockSpec((None, tr, tc), lambda i, j, sc_ref: (0, (2 * sc_ref[0] + sc_ref[1]) * nbr + i, j))

    def body(sc_ref, g_ref, r_ref, gsh_in, o_ref):
        del sc_ref, gsh_in
        own = (g_ref[...] + r_ref[6].astype(F32)) + (r_ref[2].astype(F32) + r_ref[3].astype(F32))
        o_ref[...] = own + ((r_ref[0].astype(F32) + r_ref[1].astype(F32)) + (r_ref[4].astype(F32) + r_ref[5].astype(F32)))

    return pl.pallas_call(
        body, name=name,
        grid_spec=pltpu.PrefetchScalarGridSpec(
            num_scalar_prefetch=1, grid=(nbr, nbc),
            in_specs=[g_spec, pl.BlockSpec((REDUCE_SLOTS, tr, tc), lambda i, j, sc_ref: (0, i, j)), ANY_SPEC],
            out_specs=pl.BlockSpec((None, None, tr, tc), lambda i, j, sc_ref: (l, sc_ref[1], i, j))),
        out_shape=jax.ShapeDtypeStruct(gsh.shape, F32), input_output_aliases={3: 0},
        compiler_params=_cparams(("parallel", "parallel"), 6 * _nbytes((tr, tc), F32)),
    )(sc, g, land, gsh)


def _small_allreduce(v, *, name):
    nr = v.shape[0]
    hr = nr // 2

    def body(v_ref, o_ref, pair, csum, got, ssem, rsem):
        x, y, c = _coords()
        peers = _chip_peers(x, y)
        sib = (x, y, 1 - c)
        mine = pl.ds(pl.multiple_of(c * hr, SUBLANES), hr)
        other = pl.ds(pl.multiple_of((1 - c) * hr, SUBLANES), hr)

        def rcopy(src, dst, k, dev):
            return pltpu.make_async_remote_copy(src_ref=src, dst_ref=dst, send_sem=ssem.at[k], recv_sem=rsem.at[k],
                                                device_id=dev, device_id_type=MESH)

        to_sib = rcopy(v_ref.at[other], pair, 0, sib)
        to_sib.start()
        to_sib.wait()
        csum[...] = v_ref[mine, :] + pair[...]
        sends = [rcopy(csum, got.at[k], 1 + k, (*peers[k], c)) for k in range(3)]
        for cp in sends:
            cp.start()
        for cp in sends:
            cp.wait()
        o_ref[mine, :] = (csum[...] + got[1]) + (got[0] + got[2])
        back = rcopy(o_ref.at[mine], o_ref.at[mine], 4, sib)
        back.start()
        back.wait()

    vm = pl.BlockSpec(memory_space=pltpu.VMEM)
    return pl.pallas_call(
        body, name=name, in_specs=[vm], out_specs=vm, out_shape=jax.ShapeDtypeStruct((nr, LANES), F32),
        scratch_shapes=[pltpu.VMEM((hr, LANES), F32), pltpu.VMEM((hr, LANES), F32), pltpu.VMEM((3, hr, LANES), F32),
                        pltpu.SemaphoreType.DMA((5,)), pltpu.SemaphoreType.DMA((5,))],
        compiler_params=pltpu.CompilerParams(vmem_limit_bytes=min(VMEM_CAP, 8 * _nbytes((nr, LANES), F32) + (8 << 20))),
    )(v)


def _pack(arrays):
    flat = jnp.concatenate([a.reshape(-1).astype(F32) for a in arrays])
    unit = 4 * SUBLANES * LANES
    n = -(-flat.shape[0] // unit) * unit
    return jnp.pad(flat, (0, n - flat.shape[0])).reshape(n // LANES, LANES)


def _unpack(packed, shapes):
    flat = packed.reshape(-1)
    out, off = [], 0
    for shp in shapes:
        sz = math.prod(shp)
        out.append(flat[off:off + sz].reshape(shp))
        off += sz
    return out


def _block_diag_groups(w, hpg):
    h, hd, _ = w.shape
    wg = w.reshape(h // hpg, hpg, hd, hd)
    eye = jnp.eye(hpg, dtype=w.dtype)
    return jnp.einsum("ghij,hk->ghikj", wg, eye).reshape(h // hpg, hpg * hd, hpg * hd)


def _diag_blocks(wd, hpg, hd):
    ngr = wd.shape[0]
    w5 = wd.reshape(ngr, hpg, hd, hpg, hd)
    return jnp.stack([w5[:, h, :, h, :] for h in range(hpg)], axis=1).reshape(ngr * hpg, hd, hd)


def kernel(x, norm_mix_g, norm_mlp_g, final_norm_g, a_w_in, a_conv_w, a_conv_b, a_gate_a_w, a_gate_a_b, a_gate_x_w, a_gate_x_b, a_lambda, a_w_out, b_w_in, b_norm_g, b_w_s, b_s_bias, b_w_out, c_w_in, c_conv_w, c_w_out, mlp_w1, mlp_w2, loss_target, m_norm_mix_g, m_norm_mlp_g, m_final_norm_g, m_a_w_in, m_a_conv_w, m_a_conv_b, m_a_gate_a_w, m_a_gate_a_b, m_a_gate_x_w, m_a_gate_x_b, m_a_lambda, m_a_w_out, m_b_w_in, m_b_norm_g, m_b_w_s, m_b_s_bias, m_b_w_out, m_c_w_in, m_c_conv_w, m_c_w_out, m_mlp_w1, m_mlp_w2, v_norm_mix_g, v_norm_mlp_g, v_final_norm_g, v_a_w_in, v_a_conv_w, v_a_conv_b, v_a_gate_a_w, v_a_gate_a_b, v_a_gate_x_w, v_a_gate_x_b, v_a_lambda, v_a_w_out, v_b_w_in, v_b_norm_g, v_b_w_s, v_b_s_bias, v_b_w_out, v_c_w_in, v_c_conv_w, v_c_w_out, v_mlp_w1, v_mlp_w2):
    weights = dict(norm_mix_g=norm_mix_g, norm_mlp_g=norm_mlp_g, final_norm_g=final_norm_g, a_w_in=a_w_in,
                   a_conv_w=a_conv_w, a_conv_b=a_conv_b, a_gate_a_w=a_gate_a_w, a_gate_a_b=a_gate_a_b,
                   a_gate_x_w=a_gate_x_w, a_gate_x_b=a_gate_x_b, a_lambda=a_lambda, a_w_out=a_w_out, b_w_in=b_w_in,
                   b_norm_g=b_norm_g, b_w_s=b_w_s, b_s_bias=b_s_bias, b_w_out=b_w_out, c_w_in=c_w_in,
                   c_conv_w=c_conv_w, c_w_out=c_w_out, mlp_w1=mlp_w1, mlp_w2=mlp_w2)
    mom_m = dict(norm_mix_g=m_norm_mix_g, norm_mlp_g=m_norm_mlp_g, final_norm_g=m_final_norm_g, a_w_in=m_a_w_in,
                 a_conv_w=m_a_conv_w, a_conv_b=m_a_conv_b, a_gate_a_w=m_a_gate_a_w, a_gate_a_b=m_a_gate_a_b,
                 a_gate_x_w=m_a_gate_x_w, a_gate_x_b=m_a_gate_x_b, a_lambda=m_a_lambda, a_w_out=m_a_w_out,
                 b_w_in=m_b_w_in, b_norm_g=m_b_norm_g, b_w_s=m_b_w_s, b_s_bias=m_b_s_bias, b_w_out=m_b_w_out,
                 c_w_in=m_c_w_in, c_conv_w=m_c_conv_w, c_w_out=m_c_w_out, mlp_w1=m_mlp_w1, mlp_w2=m_mlp_w2)
    mom_v = dict(norm_mix_g=v_norm_mix_g, norm_mlp_g=v_norm_mlp_g, final_norm_g=v_final_norm_g, a_w_in=v_a_w_in,
                 a_conv_w=v_a_conv_w, a_conv_b=v_a_conv_b, a_gate_a_w=v_a_gate_a_w, a_gate_a_b=v_a_gate_a_b,
                 a_gate_x_w=v_a_gate_x_w, a_gate_x_b=v_a_gate_x_b, a_lambda=v_a_lambda, a_w_out=v_a_w_out,
                 b_w_in=v_b_w_in, b_norm_g=v_b_norm_g, b_w_s=v_b_w_s, b_s_bias=v_b_s_bias, b_w_out=v_b_w_out,
                 c_w_in=v_c_w_in, c_conv_w=v_c_conv_w, c_w_out=v_c_w_out, mlp_w1=v_mlp_w1, mlp_w2=v_mlp_w2)
    order = list(weights)

    depth, d = norm_mix_g.shape
    n_a, n_b, n_c = a_w_in.shape[0], b_w_in.shape[0], c_w_in.shape[0]
    heads, hd = a_gate_a_w.shape[1], a_gate_a_w.shape[2]
    rnn = heads * hd
    gw = hd * LANES // math.gcd(hd, LANES)
    hpg = gw // hd
    assert rnn % gw == 0
    sgu_g, chunk = b_w_s.shape[1], b_w_s.shape[2]
    sgu = b_w_out.shape[1] * N_SHARDS
    gd = sgu // sgu_g
    assert gd % LANES == 0 and chunk % LANES == 0

    xi, yi, ci = _coords()
    sidx = 2 * xi + yi
    sc = jnp.stack([sidx, ci]).astype(jnp.int32)

    big = ["a_w_in", "a_w_out", "b_w_in", "b_w_out", "c_w_in", "c_w_out", "mlp_w1", "mlp_w2"]
    kinds = ["c", "r", "c", "r", "c", "r", "c", "r"]
    kind_of = dict(zip(big, kinds))
    shard_rc = {nm: weights[nm].shape[1:] for nm in big}

    def layer_keys(i):
        mixer = "abc"[i % 3]
        return [(f"{mixer}_w_in", i // 3), (f"{mixer}_w_out", i // 3)], [("mlp_w1", i), ("mlp_w2", i)]

    placed = {(nm, l): _cast_place_layer(weights[nm], l, kind_of[nm], sc, name=f"cast_place_{nm}_{l}")
              for nm in big for l in range(weights[nm].shape[0])}
    groups = list(layer_keys(0)) + [layer_keys(i)[0] + layer_keys(i)[1] for i in range(1, depth)]
    pending, token = [], None
    for gi, keys in enumerate(groups):
        plan = functools.partial(_gather_plan, [kind_of[nm] for nm, _ in keys], [shard_rc[nm] for nm, _ in keys])
        ssem, rsem, thru, token = _xfer_start([placed[key] for key in keys], plan, len(keys) * GATHER_SLOTS, token,
                                              name=f"gather_start_{gi}")
        pending.append((keys, plan, ssem, rsem, thru))
    full = {}

    def gather_wait(gi, after):
        keys, plan, ssem, rsem, thru = pending[gi]
        full.update(zip(keys, _xfer_wait(ssem, rsem, thru, plan, after, name=f"gather_wait_{gi}")))

    small_sharded = ["a_conv_w", "a_conv_b", "a_gate_a_b", "a_gate_x_b", "a_lambda", "c_conv_w"]
    mine = _pack([weights[nm] for nm in small_sharded])
    slots = jnp.zeros((N_SHARDS,) + mine.shape, F32)
    slots = lax.dynamic_update_slice(slots, jnp.where(ci == 0, mine, 0.0)[None], (sidx, 0, 0))
    slots = _small_allreduce(slots.reshape(-1, LANES), name="gather_small").reshape((N_SHARDS,) + mine.shape)
    per_chip = [_unpack(slots[j], [weights[nm].shape for nm in small_sharded]) for j in range(N_SHARDS)]
    sfull = {nm: jnp.concatenate([per_chip[j][i] for j in range(N_SHARDS)], axis=-1) for i, nm in enumerate(small_sharded)}

    wa_d = [_block_diag_groups(a_gate_a_w[j], hpg).astype(BF16) for j in range(n_a)]
    wx_d = [_block_diag_groups(a_gate_x_w[j], hpg).astype(BF16) for j in range(n_a)]
    a_vec = [jnp.concatenate([sfull["a_conv_b"][j][None], sfull["a_gate_a_b"][j][None], sfull["a_gate_x_b"][j][None],
                              sfull["a_lambda"][j][None], jnp.zeros((SUBLANES - 4, rnn), F32)]) for j in range(n_a)]
    tril = jnp.tril(jnp.ones((chunk, chunk), bool))
    wc = [jnp.where(tril[None], b_w_s[j], 0.0) for j in range(n_b)]
    wc_b = [w.astype(BF16) for w in wc]
    wct_b = [jnp.swapaxes(w, 1, 2).astype(BF16) for w in wc]
    bias_full = [jnp.repeat(b_s_bias[j].T, gd, axis=1) for j in range(n_b)]

    xs = x[0]
    tgt = loss_target[0]
    saved = []
    for i in range(depth):
        kind, j = i % 3, i // 3
        h1 = _norm_fwd(xs, norm_mix_g[i][None], name=f"norm_mix_fwd_{i}")
        gather_wait(0 if i == 0 else i + 1, h1)
        if kind == 0:
            p = _mm_nn(h1, full[("a_w_in", j)], 0, epi="plain", name=f"a_in_{i}")
            yv, hs = _a_core_fwd(p, sfull["a_conv_w"][j], a_vec[j], wa_d[j], wx_d[j], name=f"a_core_fwd_{i}")
            x1 = _mm_nn(yv, full[("a_w_out", j)], 0, epi="resid", resid=xs, name=f"a_out_{i}")
        elif kind == 1:
            p = _mm_nn(h1, full[("b_w_in", j)], 0, epi="plain", name=f"b_in_{i}")
            yv, hs = _b_core_fwd(p, b_norm_g[j][None], wc_b[j], bias_full[j], name=f"b_core_fwd_{i}"), None
            x1 = _mm_nn(yv, full[("b_w_out", j)], 0, epi="resid", resid=xs, name=f"b_out_{i}")
        else:
            p = _mm_nn(h1, full[("c_w_in", j)], 0, epi="plain", name=f"c_in_{i}")
            yv, hs = _c_core_fwd(p, sfull["c_conv_w"][j], name=f"c_core_fwd_{i}"), None
            x1 = _mm_nn(yv, full[("c_w_out", j)], 0, epi="resid", resid=xs, name=f"c_out_{i}")
        h2 = _norm_fwd(x1, norm_mlp_g[i][None], name=f"norm_mlp_fwd_{i}")
        if i == 0:
            gather_wait(1, h2)
        act, sq = _mm_nn(h2, full[("mlp_w1", i)], 0, epi="sqrelu", name=f"mlp_up_{i}")
        x2 = _mm_nn(sq, full[("mlp_w2", i)], 0, epi="resid", resid=x1, name=f"mlp_down_{i}")
        saved.append(dict(x0=xs, h1=h1, p=p, y=yv, hs=hs, x1=x1, h2=h2, act=act, sq=sq))
        xs = x2

    loss_row, dx, dxb, dg_final = _loss_and_grad(xs, final_norm_g[None], tgt, name="loss_head")
    loss = lax.psum(loss_row[0, 0], ("x", "y", "c"))

    gpart, reducing = {}, []

    def reduce_start(keys):
        plan = functools.partial(_reduce_plan, [kind_of[nm] for nm, _ in keys], [shard_rc[nm] for nm, _ in keys])
        lands = [lax.empty((REDUCE_SLOTS, shard_rc[nm][0] // 2, shard_rc[nm][1]), BF16) for nm, _ in keys]
        ssem, rsem, thru, _ = _xfer_start([gpart[key][1] for key in keys] + lands, plan, len(keys) * REDUCE_SLOTS,
                                          None, name=f"reduce_start_{len(reducing)}")
        reducing.append((keys, plan, ssem, rsem, thru))

    g_small = {}
    dg_mix, dg_mlp = [None] * depth, [None] * depth
    for i in reversed(range(depth)):
        kind, j = i % 3, i // 3
        sv = saved[i]
        mixer_keys, mlp_keys = layer_keys(i)
        dact = _mm_nt(dxb, full[("mlp_w2", i)], 0, epi="relu2grad", act=sv["act"], out_dtype=BF16,
                      name=f"mlp_down_bwd_{i}")
        gpart[("mlp_w2", i)] = _mm_tn(sv["sq"], dxb, name=f"mlp_w2_grad_{i}")
        gpart[("mlp_w1", i)] = _mm_tn(sv["h2"], dact, name=f"mlp_w1_grad_{i}")
        reduce_start(mlp_keys)
        dh2 = _mm_nt(dact, full[("mlp_w1", i)], 0, epi="plain", name=f"mlp_up_bwd_{i}")
        dx, dxb, dg_mlp[i] = _norm_bwd(dh2, sv["x1"], norm_mlp_g[i][None], dx, name=f"norm_mlp_bwd_{i}")
        if kind == 0:
            dyv = _mm_nt(dxb, full[("a_w_out", j)], 0, epi="plain", name=f"a_out_bwd_{i}")
            gpart[("a_w_out", j)] = _mm_tn(sv["y"], dxb, name=f"a_w_out_grad_{i}")
            dp, sm, dwa, dwx = _a_core_bwd(sv["p"], sv["hs"], dyv, sfull["a_conv_w"][j], a_vec[j], wa_d[j], wx_d[j],
                                           name=f"a_core_bwd_{i}")
            g_small[("a", j)] = (sm, dwa, dwx)
            w_in = "a_w_in"
        elif kind == 1:
            dyv = _mm_nt(dxb, full[("b_w_out", j)], 0, epi="plain", name=f"b_out_bwd_{i}")
            gpart[("b_w_out", j)] = _mm_tn(sv["y"], dxb, name=f"b_w_out_grad_{i}")
            dp, dng, dwc, dbf = _b_core_bwd(sv["p"], dyv, b_norm_g[j][None], wc_b[j], wct_b[j], bias_full[j],
                                            name=f"b_core_bwd_{i}")
            g_small[("b", j)] = (dng, dwc, dbf)
            w_in = "b_w_in"
        else:
            dyv = _mm_nt(dxb, full[("c_w_out", j)], 0, epi="plain", name=f"c_out_bwd_{i}")
            gpart[("c_w_out", j)] = _mm_tn(sv["y"], dxb, name=f"c_w_out_grad_{i}")
            dp, dcw = _c_core_bwd(sv["p"], dyv, sfull["c_conv_w"][j], name=f"c_core_bwd_{i}")
            g_small[("c", j)] = (dcw,)
            w_in = "c_w_in"
        gpart[(w_in, j)] = _mm_tn(sv["h1"], dp, name=f"{w_in}_grad_{i}")
        reduce_start(mixer_keys)
        dh1 = _mm_nt(dp, full[(w_in, j)], 0, epi="plain", name=f"{w_in}_bwd_{i}")
        dx, dxb, dg_mix[i] = _norm_bwd(dh1, sv["x0"], norm_mix_g[i][None], dx, name=f"norm_mix_bwd_{i}")
    grad_x = dx[None]

    gsh = {nm: lax.empty((weights[nm].shape[0], 2, shard_rc[nm][0] // 2, shard_rc[nm][1]), F32) for nm in big}
    for ri, (keys, plan, ssem, rsem, thru) in enumerate(reducing):
        lands = _xfer_wait(ssem, rsem, thru, plan, dx, name=f"reduce_wait_{ri}")[len(keys):]
        for (nm, l), land in zip(keys, lands):
            gsh[nm] = _final_sum_layer(gpart[(nm, l)][0], land, gsh[nm], l, kind_of[nm], shard_rc[nm], sc,
                                       name=f"final_sum_{nm}_{l}")
    grads = dict(zip(big, _halves_exchange([gsh[nm] for nm in big])))

    kca = a_conv_w.shape[1]
    kcc = c_conv_w.shape[1]
    small = {
        "norm_mix_g": jnp.concatenate(dg_mix), "norm_mlp_g": jnp.concatenate(dg_mlp), "final_norm_g": dg_final[0],
        "a_conv_w": jnp.stack([g_small[("a", j)][0][:kca] for j in range(n_a)]),
        "a_conv_b": jnp.stack([g_small[("a", j)][0][kca] for j in range(n_a)]),
        "a_gate_a_b": jnp.stack([g_small[("a", j)][0][kca + 1] for j in range(n_a)]),
        "a_gate_x_b": jnp.stack([g_small[("a", j)][0][kca + 2] for j in range(n_a)]),
        "a_lambda": jnp.stack([g_small[("a", j)][0][kca + 3] for j in range(n_a)]),
        "a_gate_a_w": jnp.stack([_diag_blocks(g_small[("a", j)][1], hpg, hd) for j in range(n_a)]),
        "a_gate_x_w": jnp.stack([_diag_blocks(g_small[("a", j)][2], hpg, hd) for j in range(n_a)]),
        "b_norm_g": jnp.concatenate([g_small[("b", j)][0] for j in range(n_b)]),
        "b_w_s": jnp.stack([jnp.where(tril[None], g_small[("b", j)][1], 0.0) for j in range(n_b)]),
        "b_s_bias": jnp.stack([g_small[("b", j)][2].reshape(chunk, sgu_g, gd).sum(-1).T for j in range(n_b)]),
        "c_conv_w": jnp.stack([g_small[("c", j)][0][:kcc] for j in range(n_c)]),
    }
    small_names = list(small)
    summed = _unpack(_small_allreduce(_pack([small[nm] for nm in small_names]), name="reduce_small"),
                     [small[nm].shape for nm in small_names])
    for nm, g in zip(small_names, summed):
        if nm in small_sharded:
            w_sh = weights[nm].shape[-1]
            g = lax.dynamic_slice_in_dim(g, sidx * w_sh, w_sh, axis=g.ndim - 1)
        grads[nm] = g

    deltas, new_m, new_v = {}, {}, {}
    for nm in order:
        deltas[nm], new_m[nm], new_v[nm] = _adam(grads[nm], weights[nm], mom_m[nm], mom_v[nm], name=f"adam_{nm}")
    return (loss, grad_x, *[grads[nm] for nm in order], *[deltas[nm] for nm in order],
            *[new_m[nm] for nm in order], *[new_v[nm] for nm in order])
```

```python
import functools
import math

import jax
import jax.numpy as jnp
from jax import lax
from jax.experimental import pallas as pl
from jax.experimental.pallas import tpu as pltpu

F32 = jnp.float32
BF16 = jnp.bfloat16
MESH = pl.DeviceIdType.MESH

LRU_C = 8.0
EPS = 1e-6
ADAM_LR = 0.001
ADAM_B1 = 0.9
ADAM_B2 = 0.999
ADAM_EPS = 1e-08
ADAM_WD = 0.01
ADAM_STEP = 10

N_SHARDS = 4
LANES = 128
SUBLANES = 8
V7X_VMEM_BYTES = 64 * 1024 * 1024
VMEM_CAP = V7X_VMEM_BYTES * 7 // 8
GELU_K = math.sqrt(2.0 / math.pi)
GELU_C = 0.044715
FLIPS = (2, 1, 3)


def _tile(dim, pref, mult=LANES):
    t = min(pref, dim) // mult * mult
    while t >= mult:
        if dim % t == 0:
            return t
        t -= mult
    return dim


def _nbytes(shape, dtype):
    return math.prod(shape) * jnp.dtype(dtype).itemsize


def _cparams(sem, *block_bytes, scratch=0):
    est = 2 * sum(block_bytes) + scratch + (6 << 20)
    assert est <= VMEM_CAP, est
    return pltpu.CompilerParams(dimension_semantics=sem, vmem_limit_bytes=VMEM_CAP)


def _sigmoid(x):
    return 1.0 / (1.0 + jnp.exp(-x))


def _gelu(x):
    return 0.5 * x * (1.0 + jnp.tanh(GELU_K * (x + GELU_C * x * x * x)))


def _gelu_and_grad(x):
    th = jnp.tanh(GELU_K * (x + GELU_C * x * x * x))
    g = 0.5 * x * (1.0 + th)
    dg = 0.5 * (1.0 + th) + 0.5 * x * (1.0 - th * th) * (GELU_K * (1.0 + 3.0 * GELU_C * x * x))
    return g, dg


def _softplus(x):
    z = jnp.exp(-jnp.abs(x))
    u = 1.0 + z
    l1p = jnp.where(u == 1.0, z, jnp.log(u) * z / (u - 1.0))
    return jnp.maximum(x, 0.0) + l1p


def _dot(a, b):
    return jnp.dot(a, b, preferred_element_type=F32)


def _dot_nt(a, b):
    return lax.dot_general(a, b, (((1,), (1,)), ((), ())), preferred_element_type=F32)


def _dot_tn(a, b):
    return lax.dot_general(a, b, (((0,), (0,)), ((), ())), preferred_element_type=F32)


def _row_sum(v):
    return jnp.sum(v, axis=0, keepdims=True)


MM_VMEM_BUDGET = 30 << 20
MM_SLAB_BYTES = 8 << 20


def _mm_tiles(m, k, n, out_bytes_per_elem):
    tn = n
    while True:
        for tm in (1024, 512, 256):
            if m % tm:
                continue
            est = 2 * (tm * k * 2 + k * tn * 2 + tm * tn * out_bytes_per_elem) + tm * tn * 4
            if k * tn * 2 <= MM_SLAB_BYTES and est <= MM_VMEM_BUDGET:
                return tm, tn
        nxt = _tile(n, tn - LANES)
        if nxt >= tn:
            return _tile(m, 256, SUBLANES), tn
        tn = nxt


def _mm_nn(a, w, l, *, epi, name, out_dtype=F32, resid=None, dep=None):
    m, k = a.shape
    n = w.shape[2]
    obytes = {"plain": jnp.dtype(out_dtype).itemsize, "resid": 8, "sqrelu": 4}[epi]
    tm, tn = _mm_tiles(m, k, n, obytes)
    n_in = 2 + (epi == "resid") + (dep is not None)

    def body(*refs):
        a_ref, w_ref = refs[:2]
        outs = refs[n_in:]
        v = _dot(a_ref[...], w_ref[...])
        if epi == "resid":
            outs[0][...] = refs[2][...] + v
        elif epi == "sqrelu":
            outs[0][...] = v.astype(BF16)
            rl = jnp.maximum(v, 0.0)
            outs[1][...] = (rl * rl).astype(BF16)
        else:
            outs[0][...] = v.astype(out_dtype)

    in_specs = [pl.BlockSpec((tm, k), lambda j, i: (i, 0)), pl.BlockSpec((None, k, tn), lambda j, i: (l, 0, j))]
    args = [a, w]
    o_spec = pl.BlockSpec((tm, tn), lambda j, i: (i, j))
    if epi == "resid":
        in_specs.append(o_spec)
        args.append(resid)
    if dep is not None:
        in_specs.append(pl.BlockSpec(memory_space=pl.ANY))
        args.append(dep)
    if epi == "resid":
        out_shape, out_specs = jax.ShapeDtypeStruct((m, n), F32), o_spec
    elif epi == "sqrelu":
        out_shape = (jax.ShapeDtypeStruct((m, n), BF16), jax.ShapeDtypeStruct((m, n), BF16))
        out_specs = (o_spec, o_spec)
    else:
        out_shape, out_specs = jax.ShapeDtypeStruct((m, n), out_dtype), o_spec
    blocks = [tm * k * 2, k * tn * 2, tm * tn * obytes]
    return pl.pallas_call(
        body, name=name, grid=(n // tn, m // tm), in_specs=in_specs, out_specs=out_specs, out_shape=out_shape,
        compiler_params=_cparams(("parallel", "parallel"), *blocks, scratch=3 * tm * tn * 4),
    )(*args)


def _mm_nt(dy, w, l, *, epi, name, out_dtype=F32, act=None):
    m, n = dy.shape
    k = w.shape[1]
    obytes = jnp.dtype(out_dtype).itemsize + (2 if epi == "relu2grad" else 0)
    tm, tk = _mm_tiles(m, n, k, obytes)

    def body(*refs):
        d_ref, w_ref = refs[:2]
        v = _dot_nt(d_ref[...], w_ref[...])
        if epi == "relu2grad":
            v = v * (2.0 * jnp.maximum(refs[2][...].astype(F32), 0.0))
        refs[-1][...] = v.astype(out_dtype)

    in_specs = [pl.BlockSpec((tm, n), lambda j, i: (i, 0)), pl.BlockSpec((None, tk, n), lambda j, i: (l, j, 0))]
    args = [dy, w]
    o_spec = pl.BlockSpec((tm, tk), lambda j, i: (i, j))
    if epi == "relu2grad":
        in_specs.append(o_spec)
        args.append(act)
    blocks = [tm * n * 2, tk * n * 2, tm * tk * obytes]
    return pl.pallas_call(
        body, name=name, grid=(k // tk, m // tm), in_specs=in_specs, out_specs=o_spec,
        out_shape=jax.ShapeDtypeStruct((m, k), out_dtype),
        compiler_params=_cparams(("parallel", "parallel"), *blocks, scratch=3 * tm * tk * 4),
    )(*args)


def _mm_tn(a, dy, *, name):
    m, k = a.shape
    n = dy.shape[1]
    tm, tk, tn = _tile(m, 1024, SUBLANES), _tile(k, 1024), _tile(n, 1024)
    nm = m // tm

    def body(a_ref, d_ref, o_ref, ob_ref):
        mm = pl.program_id(2)
        v = _dot_tn(a_ref[...], d_ref[...])

        @pl.when(mm == 0)
        def _():
            o_ref[...] = v

        @pl.when(mm > 0)
        def _():
            o_ref[...] += v

        @pl.when(mm == nm - 1)
        def _():
            ob_ref[...] = o_ref[...].astype(BF16)

    blocks = [_nbytes((tm, tk), BF16), _nbytes((tm, tn), BF16), 2 * _nbytes((tk, tn), F32)]
    o_spec = pl.BlockSpec((None, tk, tn), lambda i, j, mm: (0, i, j))
    return pl.pallas_call(
        body, name=name, grid=(k // tk, n // tn, nm),
        in_specs=[pl.BlockSpec((tm, tk), lambda i, j, mm: (mm, i)), pl.BlockSpec((tm, tn), lambda i, j, mm: (mm, j))],
        out_specs=(o_spec, o_spec),
        out_shape=(jax.ShapeDtypeStruct((1, k, n), F32), jax.ShapeDtypeStruct((1, k, n), BF16)),
        compiler_params=_cparams(("parallel", "parallel", "arbitrary"), *blocks, scratch=2 * _nbytes((tk, tn), F32)),
    )(a, dy)


def _norm_fwd(x, g, *, name):
    t, d = x.shape
    tt = _tile(t, 512, SUBLANES)

    def body(x_ref, g_ref, o_ref):
        xv = x_ref[...]
        r = lax.rsqrt(jnp.mean(xv * xv, axis=-1, keepdims=True) + EPS)
        o_ref[...] = ((xv * r) * g_ref[...]).astype(BF16)

    blk = pl.BlockSpec((tt, d), lambda i: (i, 0))
    return pl.pallas_call(
        body, name=name, grid=(t // tt,), in_specs=[blk, pl.BlockSpec((1, d), lambda i: (0, 0))], out_specs=blk,
        out_shape=jax.ShapeDtypeStruct((t, d), BF16),
        compiler_params=_cparams(("parallel",), 4 * _nbytes((tt, d), F32)),
    )(x, g)


def _norm_bwd(dh, x, g, dres, *, name, dep=None):
    t, d = x.shape
    tt = _tile(t, 512, SUBLANES)
    deps = [] if dep is None else [dep]

    def body(dh_ref, x_ref, g_ref, dr_ref, *rest):
        dx_ref, dxb_ref, dg_ref = rest[len(deps):]
        @pl.when(pl.program_id(0) == 0)
        def _():
            dg_ref[...] = jnp.zeros_like(dg_ref)

        xv = x_ref[...]
        dhv = dh_ref[...]
        r = lax.rsqrt(jnp.mean(xv * xv, axis=-1, keepdims=True) + EPS)
        xh = xv * r
        dhg = dhv * g_ref[...]
        dx = dr_ref[...] + r * (dhg - xh * jnp.mean(dhg * xh, axis=-1, keepdims=True))
        dx_ref[...] = dx
        dxb_ref[...] = dx.astype(BF16)
        dg_ref[...] += _row_sum(dhv * xh)

    blk = pl.BlockSpec((tt, d), lambda i: (i, 0))
    row = pl.BlockSpec((1, d), lambda i: (0, 0))
    return pl.pallas_call(
        body, name=name, grid=(t // tt,), in_specs=[blk, blk, row, blk] + [pl.BlockSpec(memory_space=pl.ANY)] * len(deps),
        out_specs=(blk, blk, row),
        out_shape=(jax.ShapeDtypeStruct((t, d), F32), jax.ShapeDtypeStruct((t, d), BF16), jax.ShapeDtypeStruct((1, d), F32)),
        compiler_params=_cparams(("arbitrary",), 8 * _nbytes((tt, d), F32)),
    )(dh, x, g, dres, *deps)


def _loss_and_grad(x, g, target, *, name):
    t, d = x.shape
    tt = _tile(t, 512, SUBLANES)
    nt = t // tt

    def body(x_ref, g_ref, t_ref, loss_ref, dx_ref, dxb_ref, dg_ref, acc):
        i = pl.program_id(0)

        @pl.when(i == 0)
        def _():
            dg_ref[...] = jnp.zeros_like(dg_ref)
            acc[...] = jnp.zeros_like(acc)

        xv = x_ref[...]
        gv = g_ref[...]
        r = lax.rsqrt(jnp.mean(xv * xv, axis=-1, keepdims=True) + EPS)
        xh = xv * r
        err = xh * gv - t_ref[...]
        acc[...] += _row_sum(err * err)
        dy = err * (1.0 / d)
        dyg = dy * gv
        dx = r * (dyg - xh * jnp.mean(dyg * xh, axis=-1, keepdims=True))
        dx_ref[...] = dx
        dxb_ref[...] = dx.astype(BF16)
        dg_ref[...] += _row_sum(dy * xh)

        @pl.when(i == nt - 1)
        def _():
            loss_ref[...] = jnp.full(loss_ref.shape, (0.5 / d) * jnp.sum(acc[...]), F32)

    blk = pl.BlockSpec((tt, d), lambda i: (i, 0))
    row = pl.BlockSpec((1, d), lambda i: (0, 0))
    return pl.pallas_call(
        body, name=name, grid=(nt,), in_specs=[blk, row, blk],
        out_specs=(pl.BlockSpec((1, LANES), lambda i: (0, 0)), blk, blk, row),
        out_shape=(jax.ShapeDtypeStruct((1, LANES), F32), jax.ShapeDtypeStruct((t, d), F32),
                   jax.ShapeDtypeStruct((t, d), BF16), jax.ShapeDtypeStruct((1, d), F32)),
        scratch_shapes=[pltpu.VMEM((1, d), F32)],
        compiler_params=_cparams(("arbitrary",), 8 * _nbytes((tt, d), F32)),
    )(x, g, target)


def _scan_fwd(a, u, tt, row):
    s = 1
    while s < tt:
        a_s = pltpu.roll(a, s, 0)
        u_s = pltpu.roll(u, s, 0)
        m = row >= s
        u = jnp.where(m, a * u_s + u, u)
        a = jnp.where(m, a * a_s, a)
        s *= 2
    return a, u


def _scan_bwd(a, u, tt, row):
    s = 1
    while s < tt:
        a_s = pltpu.roll(a, tt - s, 0)
        u_s = pltpu.roll(u, tt - s, 0)
        m = row < tt - s
        u = jnp.where(m, a * u_s + u, u)
        a = jnp.where(m, a * a_s, a)
        s *= 2
    return u


def _a_gates(xc, wa, wx, ba, bx, lam):
    xcb = xc.astype(BF16)
    ra = _sigmoid(_dot(xcb, wa) + ba)
    ia = _sigmoid(_dot(xcb, wx) + bx)
    sp = _softplus(-lam)
    la = (-LRU_C) * ra * sp
    a = jnp.exp(la)
    mult = jnp.sqrt(-jnp.tanh(la) * (a * a + 1.0))
    return xcb, ra, ia, sp, a, mult


def _a_core_fwd(p, cw, vec, wa, wx, *, name):
    t, r2 = p.shape
    r = r2 // 2
    ng, gw = wa.shape[0], wa.shape[1]
    kc = cw.shape[0]
    tt = _tile(t, 128, SUBLANES)
    nt = t // tt

    def body(p_ref, cw_ref, vec_ref, wa_ref, wx_ref, y_ref, h_ref, ext, hcar):
        i = pl.program_id(0)

        @pl.when(i == 0)
        def _():
            ext[...] = jnp.zeros_like(ext)
            hcar[...] = jnp.zeros_like(hcar)

        row = lax.broadcasted_iota(jnp.int32, (tt, gw), 0)
        for q in range(ng):
            cs = slice(q * gw, (q + 1) * gw)
            gate = p_ref[:, q * gw:(q + 1) * gw]
            xr = p_ref[:, r + q * gw:r + (q + 1) * gw]
            ext[q, SUBLANES:, :] = xr
            xc = vec_ref[0:1, cs]
            for k in range(kc):
                xc = xc + cw_ref[k:k + 1, cs] * ext[q, pl.ds(SUBLANES - (kc - 1) + k, tt), :]
            ext[q, 0:SUBLANES, :] = xr[tt - SUBLANES:, :]
            _, _, ia, _, a, mult = _a_gates(xc, wa_ref[q], wx_ref[q], vec_ref[1:2, cs], vec_ref[2:3, cs], vec_ref[3:4, cs])
            acum, hloc = _scan_fwd(a, mult * (ia * xc), tt, row)
            h = hloc + acum * hcar[0:1, cs]
            hcar[0:1, cs] = _row_sum(jnp.where(row == tt - 1, h, 0.0))
            h_ref[:, cs] = h
            y_ref[:, cs] = (h * _gelu(gate)).astype(BF16)

    blocks = [_nbytes((tt, r2), F32), _nbytes((tt, r), F32), _nbytes((tt, r), BF16), 2 * _nbytes((ng, gw, gw), BF16)]
    return pl.pallas_call(
        body, name=name, grid=(nt,),
        in_specs=[pl.BlockSpec((tt, r2), lambda i: (i, 0)), pl.BlockSpec((kc, r), lambda i: (0, 0)),
                  pl.BlockSpec((SUBLANES, r), lambda i: (0, 0)),
                  pl.BlockSpec((ng, gw, gw), lambda i: (0, 0, 0)), pl.BlockSpec((ng, gw, gw), lambda i: (0, 0, 0))],
        out_specs=(pl.BlockSpec((tt, r), lambda i: (i, 0)), pl.BlockSpec((tt, r), lambda i: (i, 0))),
        out_shape=(jax.ShapeDtypeStruct((t, r), BF16), jax.ShapeDtypeStruct((t, r), F32)),
        scratch_shapes=[pltpu.VMEM((ng, SUBLANES + tt, gw), F32), pltpu.VMEM((SUBLANES, r), F32)],
        compiler_params=_cparams(("arbitrary",), *blocks, scratch=24 * _nbytes((tt, gw), F32)),
    )(p, cw, vec, wa, wx)


def _a_core_bwd(p, hs, dy, cw, vec, wa, wx, *, name):
    t, r2 = p.shape
    r = r2 // 2
    ng, gw = wa.shape[0], wa.shape[1]
    kc = cw.shape[0]
    tt = _tile(t, 128, SUBLANES)
    nt = t // tt
    hb = tt // SUBLANES
    r_cb, r_ba, r_bx, r_lam = kc, kc + 1, kc + 2, kc + 3

    def body(p_ref, ph_ref, h_ref, hh_ref, dy_ref, cw_ref, vec_ref, wa_ref, wx_ref,
             dp_ref, sm_ref, dwa_ref, dwx_ref, ext, hext, dext, cin):
        i = pl.program_id(0)
        first = i == nt - 1
        last = i == 0

        @pl.when(last)
        def _():
            sm_ref[...] = jnp.zeros_like(sm_ref)
            dwa_ref[...] = jnp.zeros_like(dwa_ref)
            dwx_ref[...] = jnp.zeros_like(dwx_ref)
            dext[...] = jnp.zeros_like(dext)
            cin[...] = jnp.zeros_like(cin)

        row = lax.broadcasted_iota(jnp.int32, (tt, gw), 0)
        keep = jnp.where(first, 0.0, 1.0)
        for q in range(ng):
            cs = slice(q * gw, (q + 1) * gw)
            gate = p_ref[:, q * gw:(q + 1) * gw]
            xr = p_ref[:, r + q * gw:r + (q + 1) * gw]
            ext[0:SUBLANES, :] = ph_ref[:, r + q * gw:r + (q + 1) * gw] * keep
            ext[SUBLANES:, :] = xr
            xc = vec_ref[0:1, cs]
            for k in range(kc):
                xc = xc + cw_ref[k:k + 1, cs] * ext[pl.ds(SUBLANES - (kc - 1) + k, tt), :]
            lam = vec_ref[3:4, cs]
            xcb, ra, ia, sp, a, mult = _a_gates(xc, wa_ref[q], wx_ref[q], vec_ref[1:2, cs], vec_ref[2:3, cs], lam)
            hs = h_ref[:, cs]
            hext[0:SUBLANES, :] = hh_ref[:, cs] * keep
            hext[SUBLANES:, :] = hs
            hprev = hext[pl.ds(SUBLANES - 1, tt), :]
            dyv = dy_ref[:, cs]
            gl, dgl = _gelu_and_grad(gate)
            b0 = dyv * gl + jnp.where(row == tt - 1, cin[0:1, cs], 0.0)
            dh = _scan_bwd(pltpu.roll(a, tt - 1, 0), b0, tt, row)
            cin[0:1, cs] = _row_sum(jnp.where(row == 0, a * dh, 0.0))
            da = dh * hprev
            dmult = dh * (ia * xc)
            dia = dh * (mult * xc)
            dxc = dh * (mult * ia)
            dla = da * a - dmult * (a * a) / mult
            dra = dla * ((-LRU_C) * sp)
            sm_ref[r_lam:r_lam + 1, cs] += _row_sum(dla * ((-LRU_C) * ra))
            dpa = dra * ra * (1.0 - ra)
            dpx = dia * ia * (1.0 - ia)
            sm_ref[r_ba:r_ba + 1, cs] += _row_sum(dpa)
            sm_ref[r_bx:r_bx + 1, cs] += _row_sum(dpx)
            dpab = dpa.astype(BF16)
            dpxb = dpx.astype(BF16)
            dxc = dxc + _dot_nt(dpab, wa_ref[q]) + _dot_nt(dpxb, wx_ref[q])
            dwa_ref[q] += _dot_tn(xcb, dpab)
            dwx_ref[q] += _dot_tn(xcb, dpxb)
            sm_ref[r_cb:r_cb + 1, cs] += _row_sum(dxc)
            for k in range(kc):
                sm_ref[k:k + 1, cs] += _row_sum(dxc * ext[pl.ds(SUBLANES - (kc - 1) + k, tt), :])
            dext[q, 0:tt, :] = dxc
            dxr = jnp.zeros((tt, gw), F32)
            for k in range(kc):
                dxr = dxr + cw_ref[k:k + 1, cs] * dext[q, pl.ds(kc - 1 - k, tt), :]
            dext[q, tt:, :] = dxc[0:SUBLANES, :]
            dp_ref[:, q * gw:(q + 1) * gw] = (dyv * hs * dgl).astype(BF16)
            dp_ref[:, r + q * gw:r + (q + 1) * gw] = dxr.astype(BF16)

        @pl.when(first)
        def _():
            lamv = vec_ref[3:4, :]
            sm_ref[r_lam:r_lam + 1, :] = sm_ref[r_lam:r_lam + 1, :] * (-_sigmoid(-lamv))

    def tile_idx(i):
        return (nt - 1 - i, 0)

    def halo_idx(i):
        return (jnp.maximum((nt - 1 - i) * hb - 1, 0), 0)

    const2 = lambda i: (0, 0)
    const3 = lambda i: (0, 0, 0)
    blocks = [_nbytes((tt, r2), F32), 2 * _nbytes((tt, r), F32), _nbytes((tt, r2), BF16), 4 * _nbytes((ng, gw, gw), F32)]
    return pl.pallas_call(
        body, name=name, grid=(nt,),
        in_specs=[pl.BlockSpec((tt, r2), tile_idx), pl.BlockSpec((SUBLANES, r2), halo_idx),
                  pl.BlockSpec((tt, r), tile_idx), pl.BlockSpec((SUBLANES, r), halo_idx),
                  pl.BlockSpec((tt, r), tile_idx),
                  pl.BlockSpec((kc, r), const2), pl.BlockSpec((SUBLANES, r), const2),
                  pl.BlockSpec((ng, gw, gw), const3), pl.BlockSpec((ng, gw, gw), const3)],
        out_specs=(pl.BlockSpec((tt, r2), tile_idx), pl.BlockSpec((2 * SUBLANES, r), const2),
                   pl.BlockSpec((ng, gw, gw), const3), pl.BlockSpec((ng, gw, gw), const3)),
        out_shape=(jax.ShapeDtypeStruct((t, r2), BF16), jax.ShapeDtypeStruct((2 * SUBLANES, r), F32),
                   jax.ShapeDtypeStruct((ng, gw, gw), F32), jax.ShapeDtypeStruct((ng, gw, gw), F32)),
        scratch_shapes=[pltpu.VMEM((SUBLANES + tt, gw), F32), pltpu.VMEM((SUBLANES + tt, gw), F32),
                        pltpu.VMEM((ng, tt + SUBLANES, gw), F32), pltpu.VMEM((SUBLANES, r), F32)],
        compiler_params=_cparams(("arbitrary",), *blocks, scratch=40 * _nbytes((tt, gw), F32)),
    )(p, p, hs, hs, dy, cw, vec, wa, wx)


def _b_mixed(vb, wc_ref, mix_ref, tt, ch, ngr, gd):
    for n in range(tt // ch):
        for g in range(ngr):
            mix_ref[n * ch:(n + 1) * ch, g * gd:(g + 1) * gd] = _dot(wc_ref[g], vb[n * ch:(n + 1) * ch, g * gd:(g + 1) * gd])


def _b_core_fwd(p, ng_row, wc, bias_full, *, name):
    t, s2 = p.shape
    s = s2 // 2
    ngr, ch = wc.shape[0], wc.shape[1]
    gd = s // ngr
    tt = _tile(t, 2 * ch, ch)

    def body(p_ref, ng_ref, wc_ref, b_ref, y_ref, mix):
        z = _gelu(p_ref[...])
        u = z[:, :s]
        v = z[:, s:]
        rs = lax.rsqrt(jnp.mean(v * v, axis=-1, keepdims=True) + EPS)
        vb = ((v * rs) * ng_ref[...]).astype(BF16)
        _b_mixed(vb, wc_ref, mix, tt, ch, ngr, gd)
        for n in range(tt // ch):
            rows = slice(n * ch, (n + 1) * ch)
            y_ref[rows, :] = (u[rows, :] * (mix[rows, :] + b_ref[...])).astype(BF16)

    blocks = [_nbytes((tt, s2), F32), _nbytes((tt, s), BF16), _nbytes((ngr, ch, ch), BF16), _nbytes((ch, s), F32)]
    return pl.pallas_call(
        body, name=name, grid=(t // tt,),
        in_specs=[pl.BlockSpec((tt, s2), lambda i: (i, 0)), pl.BlockSpec((1, s), lambda i: (0, 0)),
                  pl.BlockSpec((ngr, ch, ch), lambda i: (0, 0, 0)), pl.BlockSpec((ch, s), lambda i: (0, 0))],
        out_specs=pl.BlockSpec((tt, s), lambda i: (i, 0)), out_shape=jax.ShapeDtypeStruct((t, s), BF16),
        scratch_shapes=[pltpu.VMEM((tt, s), F32)],
        compiler_params=_cparams(("parallel",), *blocks, scratch=12 * _nbytes((tt, s), F32)),
    )(p, ng_row, wc, bias_full)


def _b_core_bwd(p, dy, ng_row, wc, wct, bias_full, *, name):
    t, s2 = p.shape
    s = s2 // 2
    ngr, ch = wc.shape[0], wc.shape[1]
    gd = s // ngr
    tt = _tile(t, 2 * ch, ch)

    def body(p_ref, dy_ref, ng_ref, wc_ref, wct_ref, b_ref, dp_ref, dng_ref, dwc_ref, db_ref, mix, dvn):
        @pl.when(pl.program_id(0) == 0)
        def _():
            dng_ref[...] = jnp.zeros_like(dng_ref)
            dwc_ref[...] = jnp.zeros_like(dwc_ref)
            db_ref[...] = jnp.zeros_like(db_ref)

        pv = p_ref[...]
        z, dz = _gelu_and_grad(pv)
        u = z[:, :s]
        v = z[:, s:]
        rs = lax.rsqrt(jnp.mean(v * v, axis=-1, keepdims=True) + EPS)
        vh = v * rs
        ngv = ng_ref[...]
        vb = (vh * ngv).astype(BF16)
        _b_mixed(vb, wc_ref, mix, tt, ch, ngr, gd)
        dyv = dy_ref[...]
        dmx = dyv * u
        dmb = dmx.astype(BF16)
        for n in range(tt // ch):
            rows = slice(n * ch, (n + 1) * ch)
            mix[rows, :] = mix[rows, :] + b_ref[...]
            db_ref[...] += dmx[rows, :]
            for g in range(ngr):
                cols = slice(g * gd, (g + 1) * gd)
                dvn[rows, cols] = _dot(wct_ref[g], dmb[rows, cols])
                dwc_ref[g] += _dot_nt(dmb[rows, cols], vb[rows, cols])
        du = dyv * mix[...]
        dvnv = dvn[...]
        dng_ref[...] += _row_sum(dvnv * vh)
        dvh = dvnv * ngv
        dv = rs * (dvh - vh * jnp.mean(dvh * vh, axis=-1, keepdims=True))
        dp_ref[:, :s] = (du * dz[:, :s]).astype(BF16)
        dp_ref[:, s:] = (dv * dz[:, s:]).astype(BF16)

    const2 = lambda i: (0, 0)
    const3 = lambda i: (0, 0, 0)
    blocks = [_nbytes((tt, s2), F32), _nbytes((tt, s), F32), _nbytes((tt, s2), BF16),
              2 * _nbytes((ngr, ch, ch), F32), 2 * _nbytes((ch, s), F32)]
    return pl.pallas_call(
        body, name=name, grid=(t // tt,),
        in_specs=[pl.BlockSpec((tt, s2), lambda i: (i, 0)), pl.BlockSpec((tt, s), lambda i: (i, 0)),
                  pl.BlockSpec((1, s), const2), pl.BlockSpec((ngr, ch, ch), const3), pl.BlockSpec((ngr, ch, ch), const3),
                  pl.BlockSpec((ch, s), const2)],
        out_specs=(pl.BlockSpec((tt, s2), lambda i: (i, 0)), pl.BlockSpec((1, s), const2),
                   pl.BlockSpec((ngr, ch, ch), const3), pl.BlockSpec((ch, s), const2)),
        out_shape=(jax.ShapeDtypeStruct((t, s2), BF16), jax.ShapeDtypeStruct((1, s), F32),
                   jax.ShapeDtypeStruct((ngr, ch, ch), F32), jax.ShapeDtypeStruct((ch, s), F32)),
        scratch_shapes=[pltpu.VMEM((tt, s), F32), pltpu.VMEM((tt, s), F32)],
        compiler_params=_cparams(("arbitrary",), *blocks, scratch=20 * _nbytes((tt, s), F32)),
    )(p, dy, ng_row, wc, wct, bias_full)


def _c_core_fwd(p, cw, *, name):
    t, c3 = p.shape
    c = c3 // 3
    kc = cw.shape[0]
    tt = _tile(t, 256, SUBLANES)
    hb = tt // SUBLANES

    def body(p_ref, ph_ref, cw_ref, y_ref, ext):
        keep = jnp.where(pl.program_id(0) == 0, 0.0, 1.0)
        ext[0:SUBLANES, :] = ph_ref[:, c:2 * c] * ph_ref[:, 2 * c:] * keep
        ext[SUBLANES:, :] = p_ref[:, c:2 * c] * p_ref[:, 2 * c:]
        cq = jnp.zeros((tt, c), F32)
        for k in range(kc):
            cq = cq + cw_ref[k:k + 1, :] * ext[pl.ds(SUBLANES - (kc - 1) + k, tt), :]
        y_ref[...] = (p_ref[:, :c] * cq).astype(BF16)

    blocks = [_nbytes((tt, c3), F32), _nbytes((tt, c), BF16)]
    return pl.pallas_call(
        body, name=name, grid=(t // tt,),
        in_specs=[pl.BlockSpec((tt, c3), lambda i: (i, 0)),
                  pl.BlockSpec((SUBLANES, c3), lambda i: (jnp.maximum(i * hb - 1, 0), 0)),
                  pl.BlockSpec((kc, c), lambda i: (0, 0))],
        out_specs=pl.BlockSpec((tt, c), lambda i: (i, 0)), out_shape=jax.ShapeDtypeStruct((t, c), BF16),
        scratch_shapes=[pltpu.VMEM((SUBLANES + tt, c), F32)],
        compiler_params=_cparams(("parallel",), *blocks, scratch=8 * _nbytes((tt, c), F32)),
    )(p, p, cw)


def _c_core_bwd(p, dy, cw, *, name):
    t, c3 = p.shape
    c = c3 // 3
    kc = cw.shape[0]
    tt = _tile(t, 256, SUBLANES)
    hb = tt // SUBLANES
    nt = t // tt

    def body(p_ref, ph_ref, pn_ref, dy_ref, dyn_ref, cw_ref, dp_ref, dw_ref, ext, dext):
        i = pl.program_id(0)

        @pl.when(i == 0)
        def _():
            dw_ref[...] = jnp.zeros_like(dw_ref)

        keep_prev = jnp.where(i == 0, 0.0, 1.0)
        keep_next = jnp.where(i == nt - 1, 0.0, 1.0)
        gb = p_ref[:, :c]
        gc = p_ref[:, c:2 * c]
        xv = p_ref[:, 2 * c:]
        ext[0:SUBLANES, :] = ph_ref[:, c:2 * c] * ph_ref[:, 2 * c:] * keep_prev
        ext[SUBLANES:, :] = gc * xv
        dyv = dy_ref[...]
        dcq = dyv * gb
        dext[0:tt, :] = dcq
        dext[tt:, :] = dyn_ref[...] * pn_ref[:, :c] * keep_next
        cq = jnp.zeros((tt, c), F32)
        dq = jnp.zeros((tt, c), F32)
        for k in range(kc):
            tap = ext[pl.ds(SUBLANES - (kc - 1) + k, tt), :]
            cq = cq + cw_ref[k:k + 1, :] * tap
            dw_ref[k:k + 1, :] += _row_sum(dcq * tap)
            dq = dq + cw_ref[k:k + 1, :] * dext[pl.ds(kc - 1 - k, tt), :]
        dp_ref[:, :c] = (dyv * cq).astype(BF16)
        dp_ref[:, c:2 * c] = (dq * xv).astype(BF16)
        dp_ref[:, 2 * c:] = (dq * gc).astype(BF16)

    prev_idx = lambda i: (jnp.maximum(i * hb - 1, 0), 0)
    next_idx = lambda i: (jnp.minimum((i + 1) * hb, t // SUBLANES - 1), 0)
    blocks = [_nbytes((tt, c3), F32), _nbytes((tt, c), F32), _nbytes((tt, c3), BF16)]
    return pl.pallas_call(
        body, name=name, grid=(nt,),
        in_specs=[pl.BlockSpec((tt, c3), lambda i: (i, 0)), pl.BlockSpec((SUBLANES, c3), prev_idx),
                  pl.BlockSpec((SUBLANES, c3), next_idx), pl.BlockSpec((tt, c), lambda i: (i, 0)),
                  pl.BlockSpec((SUBLANES, c), next_idx), pl.BlockSpec((kc, c), lambda i: (0, 0))],
        out_specs=(pl.BlockSpec((tt, c3), lambda i: (i, 0)), pl.BlockSpec((SUBLANES, c), lambda i: (0, 0))),
        out_shape=(jax.ShapeDtypeStruct((t, c3), BF16), jax.ShapeDtypeStruct((SUBLANES, c), F32)),
        scratch_shapes=[pltpu.VMEM((SUBLANES + tt, c), F32), pltpu.VMEM((tt + SUBLANES, c), F32)],
        compiler_params=_cparams(("arbitrary",), *blocks, scratch=12 * _nbytes((tt, c), F32)),
    )(p, p, p, dy, dy, cw)


def _adam(g, w, m, v, *, name):
    shape = w.shape
    cols = shape[-1]
    rows = math.prod(shape[:-1]) if len(shape) > 1 else 1
    g2, w2, m2, v2 = (z.reshape(rows, cols) for z in (g, w, m, v))
    tr = _tile(rows, 256, SUBLANES)
    c1 = 1.0 / (1.0 - ADAM_B1 ** ADAM_STEP)
    c2 = 1.0 / (1.0 - ADAM_B2 ** ADAM_STEP)

    def body(g_ref, w_ref, m_ref, v_ref, d_ref, nm_ref, nv_ref):
        gv = g_ref[...]
        nm = ADAM_B1 * m_ref[...] + (1.0 - ADAM_B1) * gv
        nv = ADAM_B2 * v_ref[...] + (1.0 - ADAM_B2) * (gv * gv)
        d_ref[...] = -ADAM_LR * ((nm * c1) / (jnp.sqrt(nv * c2) + ADAM_EPS) + ADAM_WD * w_ref[...])
        nm_ref[...] = nm
        nv_ref[...] = nv

    blk = pl.BlockSpec((tr, cols), lambda i: (i, 0))
    sds = jax.ShapeDtypeStruct((rows, cols), F32)
    pad = _nbytes((tr, -(-cols // LANES) * LANES), F32)
    outs = pl.pallas_call(
        body, name=name, grid=(rows // tr,), in_specs=[blk] * 4, out_specs=(blk,) * 3, out_shape=(sds,) * 3,
        compiler_params=_cparams(("parallel",), 7 * pad),
    )(g2, w2, m2, v2)
    return tuple(o.reshape(shape) for o in outs)


def _coords():
    return lax.axis_index("x"), lax.axis_index("y"), lax.axis_index("c")


def _chip_peers(x, y):
    return ((1 - x, y), (x, 1 - y), (1 - x, 1 - y))


def _for_shard(s, fn):
    for j in range(N_SHARDS):
        pl.when(s == j)(functools.partial(fn, j))


def _win(ref, kind, j, h, cs):
    if kind == "c":
        return ref.at[:, h, :, pl.ds(j * cs, cs)]
    return ref.at[:, j, h]


def _cast_place(w, kind, sc, *, name):
    l, r, cs = w.shape
    tr, tc = _tile(r, 512, 16), _tile(cs, 1024)
    nb = cs // tc

    def body(sc_ref, w_ref, o_ref):
        del sc_ref
        o_ref[...] = w_ref[...].astype(BF16)

    if kind == "c":
        full_shape = (l, r, N_SHARDS * cs)
        o_spec = pl.BlockSpec((None, tr, tc), lambda lb, i, j, sc_ref: (lb, i, sc_ref[0] * nb + j))
    else:
        full_shape = (l, N_SHARDS, r, cs)
        o_spec = pl.BlockSpec((None, None, tr, tc), lambda lb, i, j, sc_ref: (lb, sc_ref[0], i, j))
    return pl.pallas_call(
        body, name=name,
        grid_spec=pltpu.PrefetchScalarGridSpec(
            num_scalar_prefetch=1, grid=(l, r // tr, nb),
            in_specs=[pl.BlockSpec((None, tr, tc), lambda lb, i, j, sc_ref: (lb, i, j))], out_specs=o_spec),
        out_shape=jax.ShapeDtypeStruct(full_shape, BF16),
        compiler_params=_cparams(("parallel", "parallel", "parallel"), 2 * _nbytes((tr, tc), F32)),
    )(sc, w)


def _gather_weights(placed, kinds, shard_shapes):
    n = len(placed)
    views = []
    for p, kind, (l, r, cs) in zip(placed, kinds, shard_shapes):
        views.append(p.reshape((l, 2, r // 2, N_SHARDS * cs) if kind == "c" else (l, N_SHARDS, 2, r // 2, cs)))

    def body(*refs):
        fu = refs[n:2 * n]
        ssem, rsem, fsem, gsem = refs[2 * n:]
        x, y, c = _coords()
        s = 2 * x + y
        peers = _chip_peers(x, y)
        sib = (x, y, 1 - c)

        def icopy(a, j, k):
            w = _win(fu[a], kinds[a], j, c, shard_shapes[a][2])
            return pltpu.make_async_remote_copy(src_ref=w, dst_ref=w, send_sem=ssem.at[a, k], recv_sem=rsem.at[a, k],
                                                device_id=(*peers[k], c), device_id_type=MESH)

        def fcopy(a, j, k, h):
            w = _win(fu[a], kinds[a], j, h, shard_shapes[a][2])
            return pltpu.make_async_remote_copy(src_ref=w, dst_ref=w, send_sem=fsem.at[a, k], recv_sem=gsem.at[a, k],
                                                device_id=sib, device_id_type=MESH)

        def run(j):
            for a in range(n):
                for k in range(3):
                    icopy(a, j, k).start()
            for k in range(3):
                for a in range(n):
                    icopy(a, j ^ FLIPS[k], k).wait_recv()
                    fcopy(a, j ^ FLIPS[k], k, c).start()
            for k in range(3):
                for a in range(n):
                    fcopy(a, j ^ FLIPS[k], k, 1 - c).wait_recv()
            for a in range(n):
                for k in range(3):
                    icopy(a, j, k).wait_send()
                    fcopy(a, j ^ FLIPS[k], k, c).wait_send()

        _for_shard(s, run)

    any_spec = pl.BlockSpec(memory_space=pl.ANY)
    outs = pl.pallas_call(
        body, name="gather_weights", in_specs=[any_spec] * n, out_specs=[any_spec] * n,
        out_shape=[jax.ShapeDtypeStruct(v.shape, BF16) for v in views],
        input_output_aliases={a: a for a in range(n)},
        scratch_shapes=[pltpu.SemaphoreType.DMA((n, 3))] * 4,
    )(*views)
    full = []
    for o, kind, (l, r, cs) in zip(outs, kinds, shard_shapes):
        full.append(o.reshape(l, r, N_SHARDS * cs) if kind == "c" else o.reshape(l, N_SHARDS * r, cs))
    return full


def _pair_exchange(grads, kinds):
    n = len(grads)
    views, half_shapes = [], []
    for g, kind in zip(grads, kinds):
        l, rr, cc = g.shape
        if kind == "c":
            views.append(g.reshape(l, 2, rr // 2, cc))
            half_shapes.append((l, rr // 2, cc))
        else:
            ks = rr // N_SHARDS
            views.append(g.reshape(l, N_SHARDS, 2, ks // 2, cc))
            half_shapes.append((l, N_SHARDS, ks // 2, cc))

    def body(*refs):
        gv, pr = refs[:n], refs[n:2 * n]
        ssem, rsem = refs[2 * n:]
        x, y, c = _coords()

        def copy(a):
            src = gv[a].at[:, 1 - c] if kinds[a] == "c" else gv[a].at[:, :, 1 - c]
            return pltpu.make_async_remote_copy(src_ref=src, dst_ref=pr[a], send_sem=ssem.at[a], recv_sem=rsem.at[a],
                                                device_id=(x, y, 1 - c), device_id_type=MESH)

        for a in range(n):
            copy(a).start()
        for a in range(n):
            copy(a).wait()

    any_spec = pl.BlockSpec(memory_space=pl.ANY)
    return pl.pallas_call(
        body, name="grad_pair_exchange", in_specs=[any_spec] * n, out_specs=[any_spec] * n,
        out_shape=[jax.ShapeDtypeStruct(hs, F32) for hs in half_shapes],
        scratch_shapes=[pltpu.SemaphoreType.DMA((n,)), pltpu.SemaphoreType.DMA((n,))],
    )(*views)


def _pair_sum(g, pair, kind, sc, *, name):
    l, rr, cc = g.shape
    if kind == "c":
        o, hr = l, rr // 2
    else:
        o, hr = l * N_SHARDS, rr // N_SHARDS // 2
    g4 = g.reshape(o, 2, hr, cc)
    p3 = pair.reshape(o, hr, cc)
    tr, tc = _tile(hr, 512, 16), _tile(cc, 1024)

    def body(sc_ref, g_ref, p_ref, o_ref):
        del sc_ref
        o_ref[...] = (g_ref[...] + p_ref[...]).astype(BF16)

    blk = pl.BlockSpec((None, tr, tc), lambda ob, i, j, sc_ref: (ob, i, j))
    out = pl.pallas_call(
        body, name=name,
        grid_spec=pltpu.PrefetchScalarGridSpec(
            num_scalar_prefetch=1, grid=(o, hr // tr, cc // tc),
            in_specs=[pl.BlockSpec((None, None, tr, tc), lambda ob, i, j, sc_ref: (ob, sc_ref[1], i, j)), blk],
            out_specs=blk),
        out_shape=jax.ShapeDtypeStruct((o, hr, cc), BF16),
        compiler_params=_cparams(("parallel", "parallel", "parallel"), 3 * _nbytes((tr, tc), F32)),
    )(sc, g4, p3)
    return out.reshape(pair.shape)


def _chip_exchange(csums, kinds):
    n = len(csums)
    views, piece_shapes = [], []
    for cs_arr, kind in zip(csums, kinds):
        if kind == "c":
            l, hr, cc = cs_arr.shape
            views.append(cs_arr)
            piece_shapes.append((l, hr, cc // N_SHARDS))
        else:
            l, _, hr, cc = cs_arr.shape
            views.append(cs_arr)
            piece_shapes.append((l, hr, cc))

    def body(*refs):
        cv, rc = refs[:n], refs[n:2 * n]
        ssem, rsem = refs[2 * n:]
        x, y, c = _coords()
        s = 2 * x + y
        peers = _chip_peers(x, y)

        def copy(a, j, k):
            if kinds[a] == "c":
                w = piece_shapes[a][2]
                src = cv[a].at[:, :, pl.ds(j * w, w)]
            else:
                src = cv[a].at[:, j]
            return pltpu.make_async_remote_copy(src_ref=src, dst_ref=rc[a].at[k], send_sem=ssem.at[a, k],
                                                recv_sem=rsem.at[a, k], device_id=(*peers[k], c), device_id_type=MESH)

        def run(j):
            for a in range(n):
                for k in range(3):
                    copy(a, j ^ FLIPS[k], k).start()
            for a in range(n):
                for k in range(3):
                    copy(a, j ^ FLIPS[k], k).wait()

        _for_shard(s, run)

    any_spec = pl.BlockSpec(memory_space=pl.ANY)
    return pl.pallas_call(
        body, name="grad_chip_exchange", in_specs=[any_spec] * n, out_specs=[any_spec] * n,
        out_shape=[jax.ShapeDtypeStruct((3, *ps), BF16) for ps in piece_shapes],
        scratch_shapes=[pltpu.SemaphoreType.DMA((n, 3)), pltpu.SemaphoreType.DMA((n, 3))],
    )(*views)


def _final_sum(g, pair, recv, kind, sc, *, name):
    l, rr, cc = g.shape
    if kind == "c":
        hr, w = rr // 2, cc // N_SHARDS
        g_v = g.reshape(l, 2, hr, cc)
        p_v = pair
        tr, tc = _tile(hr, 512, 16), _tile(w, 1024)
        nb = w // tc
        g_spec = pl.BlockSpec((None, None, tr, tc), lambda lb, i, j, sc_ref: (lb, sc_ref[1], i, sc_ref[0] * nb + j))
        p_spec = pl.BlockSpec((None, tr, tc), lambda lb, i, j, sc_ref: (lb, i, sc_ref[0] * nb + j))
    else:
        ks = rr // N_SHARDS
        hr, w = ks // 2, cc
        g_v = g.reshape(l, N_SHARDS, 2, hr, cc)
        p_v = pair
        tr, tc = _tile(hr, 512, 16), _tile(w, 1024)
        g_spec = pl.BlockSpec((None, None, None, tr, tc), lambda lb, i, j, sc_ref: (lb, sc_ref[0], sc_ref[1], i, j))
        p_spec = pl.BlockSpec((None, None, tr, tc), lambda lb, i, j, sc_ref: (lb, sc_ref[0], i, j))

    def body(sc_ref, g_ref, p_ref, r_ref, o_ref):
        del sc_ref
        own = (g_ref[...] + p_ref[...]) + r_ref[1].astype(F32)
        o_ref[...] = own + (r_ref[0].astype(F32) + r_ref[2].astype(F32))

    o_spec = pl.BlockSpec((None, None, tr, tc), lambda lb, i, j, sc_ref: (lb, sc_ref[1], i, j))
    return pl.pallas_call(
        body, name=name,
        grid_spec=pltpu.PrefetchScalarGridSpec(
            num_scalar_prefetch=1, grid=(l, hr // tr, w // tc),
            in_specs=[g_spec, p_spec, pl.BlockSpec((3, None, tr, tc), lambda lb, i, j, sc_ref: (0, lb, i, j))],
            out_specs=o_spec),
        out_shape=jax.ShapeDtypeStruct((l, 2, hr, w), F32),
        compiler_params=_cparams(("parallel", "parallel", "parallel"), 5 * _nbytes((tr, tc), F32)),
    )(sc, g_v, p_v, recv)


def _halves_exchange(halves):
    n = len(halves)

    def body(*refs):
        out = refs[n:2 * n]
        ssem, rsem = refs[2 * n:]
        x, y, c = _coords()

        def copy(a):
            return pltpu.make_async_remote_copy(src_ref=out[a].at[:, c], dst_ref=out[a].at[:, c], send_sem=ssem.at[a],
                                                recv_sem=rsem.at[a], device_id=(x, y, 1 - c), device_id_type=MESH)

        for a in range(n):
            copy(a).start()
        for a in range(n):
            copy(a).wait()

    any_spec = pl.BlockSpec(memory_space=pl.ANY)
    outs = pl.pallas_call(
        body, name="grad_halves_exchange", in_specs=[any_spec] * n, out_specs=[any_spec] * n,
        out_shape=[jax.ShapeDtypeStruct(h.shape, F32) for h in halves], input_output_aliases={a: a for a in range(n)},
        scratch_shapes=[pltpu.SemaphoreType.DMA((n,))] * 2,
    )(*halves)
    return [o.reshape(o.shape[0], 2 * o.shape[2], o.shape[3]) for o in outs]


HBM_SPEC = pl.BlockSpec(memory_space=pltpu.HBM)
SEM_SPEC = pl.BlockSpec(memory_space=pltpu.SEMAPHORE)
ANY_SPEC = pl.BlockSpec(memory_space=pl.ANY)
SIDE_EFFECT = pltpu.SideEffectType.DATAFLOW_SIDE_EFFECTING
GATHER_SLOTS = 6
REDUCE_SLOTS = 7


def _lwin(ref, kind, shard_rc, j, h):
    r, cs = shard_rc
    hr = r // 2
    if kind == "c":
        row0 = h * hr
        return ref.at[0, pl.ds(row0 if isinstance(row0, int) else pl.multiple_of(row0, 16), hr), pl.ds(j * cs, cs)]
    row0 = j * r + h * hr
    return ref.at[0, pl.ds(row0 if isinstance(row0, int) else pl.multiple_of(row0, 16), hr), :]


def _cast_place_layer(w, l, kind, sc, *, name):
    _, r, cs = w.shape
    tr, tc = _tile(r, 512, 16), _tile(cs, 1024)
    nbr, nbc = r // tr, cs // tc

    def body(sc_ref, w_ref, o_ref):
        del sc_ref
        o_ref[...] = w_ref[...].astype(BF16)

    if kind == "c":
        full_shape = (1, r, N_SHARDS * cs)
        o_spec = pl.BlockSpec((None, tr, tc), lambda i, j, sc_ref: (0, i, sc_ref[0] * nbc + j))
    else:
        full_shape = (1, N_SHARDS * r, cs)
        o_spec = pl.BlockSpec((None, tr, tc), lambda i, j, sc_ref: (0, sc_ref[0] * nbr + i, j))
    return pl.pallas_call(
        body, name=name,
        grid_spec=pltpu.PrefetchScalarGridSpec(
            num_scalar_prefetch=1, grid=(nbr, nbc),
            in_specs=[pl.BlockSpec((None, tr, tc), lambda i, j, sc_ref: (l, i, j))], out_specs=o_spec),
        out_shape=jax.ShapeDtypeStruct(full_shape, BF16),
        compiler_params=_cparams(("parallel", "parallel"), 2 * _nbytes((tr, tc), F32)),
    )(sc, w)


def _gather_plan(kinds, shards, refs, j, c, peers, sib):
    del sib
    sends, recvs = [], []
    for a, ref in enumerate(refs):
        mine = _lwin(ref, kinds[a], shards[a], j, c)
        for k in range(3):
            for h in range(2):
                base = a * GATHER_SLOTS + 2 * k
                sends.append((mine, mine, (*peers[k], h), base + h, base + c))
                recvs.append((_lwin(ref, kinds[a], shards[a], j ^ FLIPS[k], h), base + h))
    return sends, recvs


def _reduce_plan(kinds, shards, refs, j, c, peers, sib):
    n = len(refs) // 2
    sends, recvs = [], []
    for a in range(n):
        part, land = refs[a], refs[n + a]
        for k in range(3):
            for h in range(2):
                src = _lwin(part, kinds[a], shards[a], j ^ FLIPS[k], h)
                base = a * REDUCE_SLOTS + 2 * k
                sends.append((src, land.at[2 * k + c], (*peers[k], h), base + h, base + c))
        sends.append((_lwin(part, kinds[a], shards[a], j, 1 - c), land.at[6], sib, a * REDUCE_SLOTS + 6,
                      a * REDUCE_SLOTS + 6))
        recvs += [(land.at[t], a * REDUCE_SLOTS + t) for t in range(REDUCE_SLOTS)]
    return sends, recvs


def _xfer_start(arrays, plan, n_sems, after, *, name):
    n = len(arrays)
    n_after = 0 if after is None else 1

    def body(*refs):
        ins = refs[:n]
        ssem, rsem = refs[n + n_after], refs[n + n_after + 1]
        token = refs[-1]
        x, y, c = _coords()
        peers = _chip_peers(x, y)

        def run(j):
            for src, dst, dev, si, ri in plan(ins, j, c, peers, (x, y, 1 - c))[0]:
                pltpu.make_async_remote_copy(src_ref=src, dst_ref=dst, send_sem=ssem.at[si], recv_sem=rsem.at[ri],
                                             device_id=dev, device_id_type=MESH).start()

        _for_shard(2 * x + y, run)
        token[...] = jnp.zeros_like(token)

    sem_t = pltpu.SemaphoreType.DMA((n_sems,))
    outs = pl.pallas_call(
        body, name=name,
        out_shape=(sem_t, sem_t, *[pltpu.HBM(v.shape, v.dtype) for v in arrays],
                   jax.ShapeDtypeStruct((SUBLANES, LANES), F32)),
        in_specs=[HBM_SPEC] * n + [ANY_SPEC] * n_after,
        out_specs=(SEM_SPEC, SEM_SPEC, *[HBM_SPEC] * n, pl.BlockSpec(memory_space=pltpu.VMEM)),
        input_output_aliases={a: 2 + a for a in range(n)},
        compiler_params=pltpu.CompilerParams(has_side_effects=SIDE_EFFECT),
    )(*[pltpu.with_memory_space_constraint(v, pltpu.HBM) for v in arrays], *([] if after is None else [after]))
    return outs[0], outs[1], list(outs[2:2 + n]), outs[-1]


def _xfer_wait(ssem, rsem, arrays, plan, after, *, name):
    n = len(arrays)

    def body(*refs):
        ins = refs[:n]
        ssem_ref, rsem_ref = refs[n], refs[n + 1]
        x, y, c = _coords()
        peers = _chip_peers(x, y)

        def run(j):
            sends, recvs = plan(ins, j, c, peers, (x, y, 1 - c))
            for src, dst, dev, si, ri in sends:
                pltpu.make_async_remote_copy(src_ref=src, dst_ref=dst, send_sem=ssem_ref.at[si], recv_sem=rsem_ref.at[ri],
                                             device_id=dev, device_id_type=MESH).wait_send()
            for dst, ri in recvs:
                pltpu.make_async_remote_copy(src_ref=dst, dst_ref=dst, send_sem=ssem_ref.at[ri], recv_sem=rsem_ref.at[ri],
                                             device_id=(x, y, c), device_id_type=MESH).wait_recv()

        _for_shard(2 * x + y, run)

    outs = pl.pallas_call(
        body, name=name, out_shape=[pltpu.HBM(v.shape, v.dtype) for v in arrays],
        in_specs=[HBM_SPEC] * n + [SEM_SPEC, SEM_SPEC, ANY_SPEC], out_specs=[HBM_SPEC] * n,
        input_output_aliases={a: a for a in range(n)},
        compiler_params=pltpu.CompilerParams(has_side_effects=SIDE_EFFECT),
    )(*arrays, ssem, rsem, after)
    return list(outs)


def _final_sum_layer(g, land, gsh, l, kind, shard_rc, sc, *, name):
    r, cs = shard_rc
    hr = r // 2
    tr, tc = _tile(hr, 512, 16), _tile(cs, 1024)
    nbr, nbc = hr // tr, cs // tc
    if kind == "c":
        g_spec = pl.BlockSpec((None, tr, tc), lambda i, j, sc_ref: (0, sc_ref[1] * nbr + i, sc_ref[0] * nbc + j))
    else:
        g_spec = pl.BlockSpec((None, tr, tc), lambda i, j, sc_ref: (0, (2 * sc_ref[0] + sc_ref[1]) * nbr + i, j))

    def body(sc_ref, g_ref, r_ref, gsh_in, o_ref):
        del sc_ref, gsh_in
        own = (g_ref[...] + r_ref[6].astype(F32)) + (r_ref[2].astype(F32) + r_ref[3].astype(F32))
        o_ref[...] = own + ((r_ref[0].astype(F32) + r_ref[1].astype(F32)) + (r_ref[4].astype(F32) + r_ref[5].astype(F32)))

    return pl.pallas_call(
        body, name=name,
        grid_spec=pltpu.PrefetchScalarGridSpec(
            num_scalar_prefetch=1, grid=(nbr, nbc),
            in_specs=[g_spec, pl.BlockSpec((REDUCE_SLOTS, tr, tc), lambda i, j, sc_ref: (0, i, j)), ANY_SPEC],
            out_specs=pl.BlockSpec((None, None, tr, tc), lambda i, j, sc_ref: (l, sc_ref[1], i, j))),
        out_shape=jax.ShapeDtypeStruct(gsh.shape, F32), input_output_aliases={3: 0},
        compiler_params=_cparams(("parallel", "parallel"), 6 * _nbytes((tr, tc), F32)),
    )(sc, g, land, gsh)


def _small_allreduce(v, *, name):
    nr = v.shape[0]
    hr = nr // 2

    def body(v_ref, o_ref, pair, csum, got, ssem, rsem):
        x, y, c = _coords()
        peers = _chip_peers(x, y)
        sib = (x, y, 1 - c)
        mine = pl.ds(pl.multiple_of(c * hr, SUBLANES), hr)
        other = pl.ds(pl.multiple_of((1 - c) * hr, SUBLANES), hr)

        def rcopy(src, dst, k, dev):
            return pltpu.make_async_remote_copy(src_ref=src, dst_ref=dst, send_sem=ssem.at[k], recv_sem=rsem.at[k],
                                                device_id=dev, device_id_type=MESH)

        to_sib = rcopy(v_ref.at[other], pair, 0, sib)
        to_sib.start()
        to_sib.wait()
        csum[...] = v_ref[mine, :] + pair[...]
        sends = [rcopy(csum, got.at[k], 1 + k, (*peers[k], c)) for k in range(3)]
        for cp in sends:
            cp.start()
        for cp in sends:
            cp.wait()
        o_ref[mine, :] = (csum[...] + got[1]) + (got[0] + got[2])
        back = rcopy(o_ref.at[mine], o_ref.at[mine], 4, sib)
        back.start()
        back.wait()

    vm = pl.BlockSpec(memory_space=pltpu.VMEM)
    return pl.pallas_call(
        body, name=name, in_specs=[vm], out_specs=vm, out_shape=jax.ShapeDtypeStruct((nr, LANES), F32),
        scratch_shapes=[pltpu.VMEM((hr, LANES), F32), pltpu.VMEM((hr, LANES), F32), pltpu.VMEM((3, hr, LANES), F32),
                        pltpu.SemaphoreType.DMA((5,)), pltpu.SemaphoreType.DMA((5,))],
        compiler_params=pltpu.CompilerParams(vmem_limit_bytes=min(VMEM_CAP, 8 * _nbytes((nr, LANES), F32) + (8 << 20))),
    )(v)


def _pack(arrays):
    flat = jnp.concatenate([a.reshape(-1).astype(F32) for a in arrays])
    unit = 4 * SUBLANES * LANES
    n = -(-flat.shape[0] // unit) * unit
    return jnp.pad(flat, (0, n - flat.shape[0])).reshape(n // LANES, LANES)


def _unpack(packed, shapes):
    flat = packed.reshape(-1)
    out, off = [], 0
    for shp in shapes:
        sz = math.prod(shp)
        out.append(flat[off:off + sz].reshape(shp))
        off += sz
    return out


def _block_diag_groups(w, hpg):
    h, hd, _ = w.shape
    wg = w.reshape(h // hpg, hpg, hd, hd)
    eye = jnp.eye(hpg, dtype=w.dtype)
    return jnp.einsum("ghij,hk->ghikj", wg, eye).reshape(h // hpg, hpg * hd, hpg * hd)


def _diag_blocks(wd, hpg, hd):
    ngr = wd.shape[0]
    w5 = wd.reshape(ngr, hpg, hd, hpg, hd)
    return jnp.stack([w5[:, h, :, h, :] for h in range(hpg)], axis=1).reshape(ngr * hpg, hd, hd)


def kernel(x, norm_mix_g, norm_mlp_g, final_norm_g, a_w_in, a_conv_w, a_conv_b, a_gate_a_w, a_gate_a_b, a_gate_x_w, a_gate_x_b, a_lambda, a_w_out, b_w_in, b_norm_g, b_w_s, b_s_bias, b_w_out, c_w_in, c_conv_w, c_w_out, mlp_w1, mlp_w2, loss_target, m_norm_mix_g, m_norm_mlp_g, m_final_norm_g, m_a_w_in, m_a_conv_w, m_a_conv_b, m_a_gate_a_w, m_a_gate_a_b, m_a_gate_x_w, m_a_gate_x_b, m_a_lambda, m_a_w_out, m_b_w_in, m_b_norm_g, m_b_w_s, m_b_s_bias, m_b_w_out, m_c_w_in, m_c_conv_w, m_c_w_out, m_mlp_w1, m_mlp_w2, v_norm_mix_g, v_norm_mlp_g, v_final_norm_g, v_a_w_in, v_a_conv_w, v_a_conv_b, v_a_gate_a_w, v_a_gate_a_b, v_a_gate_x_w, v_a_gate_x_b, v_a_lambda, v_a_w_out, v_b_w_in, v_b_norm_g, v_b_w_s, v_b_s_bias, v_b_w_out, v_c_w_in, v_c_conv_w, v_c_w_out, v_mlp_w1, v_mlp_w2):
    weights = dict(norm_mix_g=norm_mix_g, norm_mlp_g=norm_mlp_g, final_norm_g=final_norm_g, a_w_in=a_w_in,
                   a_conv_w=a_conv_w, a_conv_b=a_conv_b, a_gate_a_w=a_gate_a_w, a_gate_a_b=a_gate_a_b,
                   a_gate_x_w=a_gate_x_w, a_gate_x_b=a_gate_x_b, a_lambda=a_lambda, a_w_out=a_w_out, b_w_in=b_w_in,
                   b_norm_g=b_norm_g, b_w_s=b_w_s, b_s_bias=b_s_bias, b_w_out=b_w_out, c_w_in=c_w_in,
                   c_conv_w=c_conv_w, c_w_out=c_w_out, mlp_w1=mlp_w1, mlp_w2=mlp_w2)
    mom_m = dict(norm_mix_g=m_norm_mix_g, norm_mlp_g=m_norm_mlp_g, final_norm_g=m_final_norm_g, a_w_in=m_a_w_in,
                 a_conv_w=m_a_conv_w, a_conv_b=m_a_conv_b, a_gate_a_w=m_a_gate_a_w, a_gate_a_b=m_a_gate_a_b,
                 a_gate_x_w=m_a_gate_x_w, a_gate_x_b=m_a_gate_x_b, a_lambda=m_a_lambda, a_w_out=m_a_w_out,
                 b_w_in=m_b_w_in, b_norm_g=m_b_norm_g, b_w_s=m_b_w_s, b_s_bias=m_b_s_bias, b_w_out=m_b_w_out,
                 c_w_in=m_c_w_in, c_conv_w=m_c_conv_w, c_w_out=m_c_w_out, mlp_w1=m_mlp_w1, mlp_w2=m_mlp_w2)
    mom_v = dict(norm_mix_g=v_norm_mix_g, norm_mlp_g=v_norm_mlp_g, final_norm_g=v_final_norm_g, a_w_in=v_a_w_in,
                 a_conv_w=v_a_conv_w, a_conv_b=v_a_conv_b, a_gate_a_w=v_a_gate_a_w, a_gate_a_b=v_a_gate_a_b,
                 a_gate_x_w=v_a_gate_x_w, a_gate_x_b=v_a_gate_x_b, a_lambda=v_a_lambda, a_w_out=v_a_w_out,
                 b_w_in=v_b_w_in, b_norm_g=v_b_norm_g, b_w_s=v_b_w_s, b_s_bias=v_b_s_bias, b_w_out=v_b_w_out,
                 c_w_in=v_c_w_in, c_conv_w=v_c_conv_w, c_w_out=v_c_w_out, mlp_w1=v_mlp_w1, mlp_w2=v_mlp_w2)
    order = list(weights)

    depth, d = norm_mix_g.shape
    n_a, n_b, n_c = a_w_in.shape[0], b_w_in.shape[0], c_w_in.shape[0]
    heads, hd = a_gate_a_w.shape[1], a_gate_a_w.shape[2]
    rnn = heads * hd
    gw = hd * LANES // math.gcd(hd, LANES)
    hpg = gw // hd
    assert rnn % gw == 0
    sgu_g, chunk = b_w_s.shape[1], b_w_s.shape[2]
    sgu = b_w_out.shape[1] * N_SHARDS
    gd = sgu // sgu_g
    assert gd % LANES == 0 and chunk % LANES == 0

    xi, yi, ci = _coords()
    sidx = 2 * xi + yi
    sc = jnp.stack([sidx, ci]).astype(jnp.int32)

    big = ["a_w_in", "a_w_out", "b_w_in", "b_w_out", "c_w_in", "c_w_out", "mlp_w1", "mlp_w2"]
    kinds = ["c", "r", "c", "r", "c", "r", "c", "r"]
    kind_of = dict(zip(big, kinds))
    shard_rc = {nm: weights[nm].shape[1:] for nm in big}

    def layer_keys(i):
        mixer = "abc"[i % 3]
        return [(f"{mixer}_w_in", i // 3), (f"{mixer}_w_out", i // 3)], [("mlp_w1", i), ("mlp_w2", i)]

    placed = {(nm, l): _cast_place_layer(weights[nm], l, kind_of[nm], sc, name=f"cast_place_{nm}_{l}")
              for nm in big for l in range(weights[nm].shape[0])}
    groups = [keys for i in range(depth) for keys in layer_keys(i)]
    full, inflight = {}, {}

    def gather_start(gi, after):
        keys = groups[gi]
        plan = functools.partial(_gather_plan, [kind_of[nm] for nm, _ in keys], [shard_rc[nm] for nm, _ in keys])
        ssem, rsem, thru, token = _xfer_start([placed[key] for key in keys], plan, len(keys) * GATHER_SLOTS, after,
                                              name=f"gather_start_{gi}")
        inflight[gi] = (plan, ssem, rsem, thru)
        return token

    def gather_next(gi, after):
        plan, ssem, rsem, thru = inflight.pop(gi)
        arrs = _xfer_wait(ssem, rsem, thru, plan, after, name=f"gather_wait_{gi}")
        full.update(zip(groups[gi], arrs))
        return gather_start(gi + 1, arrs[0]) if gi + 1 < len(groups) else None

    gather_start(0, None)

    small_sharded = ["a_conv_w", "a_conv_b", "a_gate_a_b", "a_gate_x_b", "a_lambda", "c_conv_w"]
    mine = _pack([weights[nm] for nm in small_sharded])
    slots = jnp.zeros((N_SHARDS,) + mine.shape, F32)
    slots = lax.dynamic_update_slice(slots, jnp.where(ci == 0, mine, 0.0)[None], (sidx, 0, 0))
    slots = _small_allreduce(slots.reshape(-1, LANES), name="gather_small").reshape((N_SHARDS,) + mine.shape)
    per_chip = [_unpack(slots[j], [weights[nm].shape for nm in small_sharded]) for j in range(N_SHARDS)]
    sfull = {nm: jnp.concatenate([per_chip[j][i] for j in range(N_SHARDS)], axis=-1) for i, nm in enumerate(small_sharded)}

    wa_d = [_block_diag_groups(a_gate_a_w[j], hpg).astype(BF16) for j in range(n_a)]
    wx_d = [_block_diag_groups(a_gate_x_w[j], hpg).astype(BF16) for j in range(n_a)]
    a_vec = [jnp.concatenate([sfull["a_conv_b"][j][None], sfull["a_gate_a_b"][j][None], sfull["a_gate_x_b"][j][None],
                              sfull["a_lambda"][j][None], jnp.zeros((SUBLANES - 4, rnn), F32)]) for j in range(n_a)]
    tril = jnp.tril(jnp.ones((chunk, chunk), bool))
    wc = [jnp.where(tril[None], b_w_s[j], 0.0) for j in range(n_b)]
    wc_b = [w.astype(BF16) for w in wc]
    wct_b = [jnp.swapaxes(w, 1, 2).astype(BF16) for w in wc]
    bias_full = [jnp.repeat(b_s_bias[j].T, gd, axis=1) for j in range(n_b)]

    xs = x[0]
    tgt = loss_target[0]
    saved = []
    for i in range(depth):
        kind, j = i % 3, i // 3
        h1 = _norm_fwd(xs, norm_mix_g[i][None], name=f"norm_mix_fwd_{i}")
        tok = gather_next(2 * i, h1)
        if kind == 0:
            p = _mm_nn(h1, full[("a_w_in", j)], 0, epi="plain", name=f"a_in_{i}", dep=tok)
            yv, hs = _a_core_fwd(p, sfull["a_conv_w"][j], a_vec[j], wa_d[j], wx_d[j], name=f"a_core_fwd_{i}")
            x1 = _mm_nn(yv, full[("a_w_out", j)], 0, epi="resid", resid=xs, name=f"a_out_{i}")
        elif kind == 1:
            p = _mm_nn(h1, full[("b_w_in", j)], 0, epi="plain", name=f"b_in_{i}", dep=tok)
            yv, hs = _b_core_fwd(p, b_norm_g[j][None], wc_b[j], bias_full[j], name=f"b_core_fwd_{i}"), None
            x1 = _mm_nn(yv, full[("b_w_out", j)], 0, epi="resid", resid=xs, name=f"b_out_{i}")
        else:
            p = _mm_nn(h1, full[("c_w_in", j)], 0, epi="plain", name=f"c_in_{i}", dep=tok)
            yv, hs = _c_core_fwd(p, sfull["c_conv_w"][j], name=f"c_core_fwd_{i}"), None
            x1 = _mm_nn(yv, full[("c_w_out", j)], 0, epi="resid", resid=xs, name=f"c_out_{i}")
        h2 = _norm_fwd(x1, norm_mlp_g[i][None], name=f"norm_mlp_fwd_{i}")
        tok = gather_next(2 * i + 1, h2)
        act, sq = _mm_nn(h2, full[("mlp_w1", i)], 0, epi="sqrelu", name=f"mlp_up_{i}", dep=tok)
        x2 = _mm_nn(sq, full[("mlp_w2", i)], 0, epi="resid", resid=x1, name=f"mlp_down_{i}")
        saved.append(dict(x0=xs, h1=h1, p=p, y=yv, hs=hs, x1=x1, h2=h2, act=act, sq=sq))
        xs = x2

    loss_row, dx, dxb, dg_final = _loss_and_grad(xs, final_norm_g[None], tgt, name="loss_head")
    loss = lax.psum(loss_row[0, 0], ("x", "y", "c"))

    gpart, reducing = {}, []

    def reduce_start(keys):
        plan = functools.partial(_reduce_plan, [kind_of[nm] for nm, _ in keys], [shard_rc[nm] for nm, _ in keys])
        lands = [lax.empty((REDUCE_SLOTS, shard_rc[nm][0] // 2, shard_rc[nm][1]), BF16) for nm, _ in keys]
        ssem, rsem, thru, token = _xfer_start([gpart[key][1] for key in keys] + lands, plan, len(keys) * REDUCE_SLOTS,
                                              None, name=f"reduce_start_{len(reducing)}")
        reducing.append((keys, plan, ssem, rsem, thru))
        return token

    g_small = {}
    dg_mix, dg_mlp = [None] * depth, [None] * depth
    for i in reversed(range(depth)):
        kind, j = i % 3, i // 3
        sv = saved[i]
        mixer_keys, mlp_keys = layer_keys(i)
        dact = _mm_nt(dxb, full[("mlp_w2", i)], 0, epi="relu2grad", act=sv["act"], out_dtype=BF16,
                      name=f"mlp_down_bwd_{i}")
        gpart[("mlp_w2", i)] = _mm_tn(sv["sq"], dxb, name=f"mlp_w2_grad_{i}")
        gpart[("mlp_w1", i)] = _mm_tn(sv["h2"], dact, name=f"mlp_w1_grad_{i}")
        tok = reduce_start(mlp_keys)
        dh2 = _mm_nt(dact, full[("mlp_w1", i)], 0, epi="plain", name=f"mlp_up_bwd_{i}")
        dx, dxb, dg_mlp[i] = _norm_bwd(dh2, sv["x1"], norm_mlp_g[i][None], dx, name=f"norm_mlp_bwd_{i}", dep=tok)
        if kind == 0:
            dyv = _mm_nt(dxb, full[("a_w_out", j)], 0, epi="plain", name=f"a_out_bwd_{i}")
            gpart[("a_w_out", j)] = _mm_tn(sv["y"], dxb, name=f"a_w_out_grad_{i}")
            dp, sm, dwa, dwx = _a_core_bwd(sv["p"], sv["hs"], dyv, sfull["a_conv_w"][j], a_vec[j], wa_d[j], wx_d[j],
                                           name=f"a_core_bwd_{i}")
            g_small[("a", j)] = (sm, dwa, dwx)
            w_in = "a_w_in"
        elif kind == 1:
            dyv = _mm_nt(dxb, full[("b_w_out", j)], 0, epi="plain", name=f"b_out_bwd_{i}")
            gpart[("b_w_out", j)] = _mm_tn(sv["y"], dxb, name=f"b_w_out_grad_{i}")
            dp, dng, dwc, dbf = _b_core_bwd(sv["p"], dyv, b_norm_g[j][None], wc_b[j], wct_b[j], bias_full[j],
                                            name=f"b_core_bwd_{i}")
            g_small[("b", j)] = (dng, dwc, dbf)
            w_in = "b_w_in"
        else:
            dyv = _mm_nt(dxb, full[("c_w_out", j)], 0, epi="plain", name=f"c_out_bwd_{i}")
            gpart[("c_w_out", j)] = _mm_tn(sv["y"], dxb, name=f"c_w_out_grad_{i}")
            dp, dcw = _c_core_bwd(sv["p"], dyv, sfull["c_conv_w"][j], name=f"c_core_bwd_{i}")
            g_small[("c", j)] = (dcw,)
            w_in = "c_w_in"
        gpart[(w_in, j)] = _mm_tn(sv["h1"], dp, name=f"{w_in}_grad_{i}")
        tok = reduce_start(mixer_keys)
        dh1 = _mm_nt(dp, full[(w_in, j)], 0, epi="plain", name=f"{w_in}_bwd_{i}")
        dx, dxb, dg_mix[i] = _norm_bwd(dh1, sv["x0"], norm_mix_g[i][None], dx, name=f"norm_mix_bwd_{i}", dep=tok)
    grad_x = dx[None]

    gsh = {nm: lax.empty((weights[nm].shape[0], 2, shard_rc[nm][0] // 2, shard_rc[nm][1]), F32) for nm in big}
    for ri, (keys, plan, ssem, rsem, thru) in enumerate(reducing):
        lands = _xfer_wait(ssem, rsem, thru, plan, dx, name=f"reduce_wait_{ri}")[len(keys):]
        for (nm, l), land in zip(keys, lands):
            gsh[nm] = _final_sum_layer(gpart[(nm, l)][0], land, gsh[nm], l, kind_of[nm], shard_rc[nm], sc,
                                       name=f"final_sum_{nm}_{l}")
    grads = dict(zip(big, _halves_exchange([gsh[nm] for nm in big])))

    kca = a_conv_w.shape[1]
    kcc = c_conv_w.shape[1]
    small = {
        "norm_mix_g": jnp.concatenate(dg_mix), "norm_mlp_g": jnp.concatenate(dg_mlp), "final_norm_g": dg_final[0],
        "a_conv_w": jnp.stack([g_small[("a", j)][0][:kca] for j in range(n_a)]),
        "a_conv_b": jnp.stack([g_small[("a", j)][0][kca] for j in range(n_a)]),
        "a_gate_a_b": jnp.stack([g_small[("a", j)][0][kca + 1] for j in range(n_a)]),
        "a_gate_x_b": jnp.stack([g_small[("a", j)][0][kca + 2] for j in range(n_a)]),
        "a_lambda": jnp.stack([g_small[("a", j)][0][kca + 3] for j in range(n_a)]),
        "a_gate_a_w": jnp.stack([_diag_blocks(g_small[("a", j)][1], hpg, hd) for j in range(n_a)]),
        "a_gate_x_w": jnp.stack([_diag_blocks(g_small[("a", j)][2], hpg, hd) for j in range(n_a)]),
        "b_norm_g": jnp.concatenate([g_small[("b", j)][0] for j in range(n_b)]),
        "b_w_s": jnp.stack([jnp.where(tril[None], g_small[("b", j)][1], 0.0) for j in range(n_b)]),
        "b_s_bias": jnp.stack([g_small[("b", j)][2].reshape(chunk, sgu_g, gd).sum(-1).T for j in range(n_b)]),
        "c_conv_w": jnp.stack([g_small[("c", j)][0][:kcc] for j in range(n_c)]),
    }
    small_names = list(small)
    summed = _unpack(_small_allreduce(_pack([small[nm] for nm in small_names]), name="reduce_small"),
                     [small[nm].shape for nm in small_names])
    for nm, g in zip(small_names, summed):
        if nm in small_sharded:
            w_sh = weights[nm].shape[-1]
            g = lax.dynamic_slice_in_dim(g, sidx * w_sh, w_sh, axis=g.ndim - 1)
        grads[nm] = g

    deltas, new_m, new_v = {}, {}, {}
    for nm in order:
        deltas[nm], new_m[nm], new_v[nm] = _adam(grads[nm], weights[nm], mom_m[nm], mom_v[nm], name=f"adam_{nm}")
    return (loss, grad_x, *[grads[nm] for nm in order], *[deltas[nm] for nm in order],
            *[new_m[nm] for nm in order], *[new_v[nm] for nm in order])
```
